```python
import jax, jax.numpy as jnp
from jax import lax
import numpy as np

D_MODEL = 1024
BATCH = 2
SEQ = 8192
DEPTH = 2
DEC_BATCH = 8
DEC_SEQ = 32
PAST_LEN = 4096

CHUNK = 64
POOL_WINDOWS = (2, 4, 8, 16)
POOL_GROUPS = 4
POOL_GW = 64
POOL_W = POOL_GROUPS * POOL_GW
POOL_STATE = max(POOL_WINDOWS) - 1
SGU_CHUNK = 128
SGU_GROUPS = 4
SGU_GW = 64
SGU_W = SGU_GROUPS * SGU_GW
ATT_HEADS = 8
HEAD_DIM = 64
ATT_W = ATT_HEADS * HEAD_DIM
BAND_CHUNKS = 8
BAND = BAND_CHUNKS * CHUNK
REL_CLIP = 128
CACHE_ROWS = min(BAND, PAST_LEN)
IN_W = POOL_W + 2 * SGU_W + 3 * ATT_W
SPLITS = [POOL_W, POOL_W + SGU_W, POOL_W + 2 * SGU_W, POOL_W + 2 * SGU_W + ATT_W, POOL_W + 2 * SGU_W + 2 * ATT_W]
N_BRANCH = 3
PEER_HEADS = 8
N_KEYS = 128
N_EXPERTS = N_KEYS * N_KEYS
PK_HALF = 128
PK_DIM = 2 * PK_HALF
PK_TOPK = 16
PEER_BLOCK = 256
EPS = 1e-6

kernel_name = 'hybrid_stream_pool_sgu_band_peer_step'


def rmsnorm(x, g):
    xf = x.astype(jnp.float32)
    y = xf * lax.rsqrt(jnp.mean(xf * xf, axis=-1, keepdims=True) + EPS)
    return (y * g.astype(jnp.float32)).astype(x.dtype)


def pool_mix(a, prefix, w_pool, s_pool):
    B, T, _ = a.shape
    ap = jnp.concatenate([prefix.astype(a.dtype), a], axis=1)
    cs = jnp.cumsum(ap.astype(jnp.float32), axis=1)
    cs = jnp.concatenate([jnp.zeros((B, 1, POOL_W), jnp.float32), cs], axis=1)
    end = cs[:, POOL_STATE + 1:]
    means = []
    for g, w in enumerate(POOL_WINDOWS):
        sl = slice(g * POOL_GW, (g + 1) * POOL_GW)
        start = lax.slice_in_dim(cs, POOL_STATE + 1 - w, POOL_STATE + 1 - w + T, axis=1)
        means.append((end[..., sl] - start[..., sl]) / w)
    pooled = (jnp.concatenate(means, axis=-1) - a.astype(jnp.float32)).astype(a.dtype)
    pooled = pooled.reshape(B, T, POOL_GROUPS, POOL_GW)
    y = jnp.einsum('btgc,gcd->btgd', pooled, w_pool).reshape(B, T, POOL_W)
    return y * s_pool, ap[:, -POOL_STATE:]


def sgu_mix(u, v, g_v, w_s, b_s):
    B, T, _ = v.shape
    L = min(T, SGU_CHUNK)
    vn = rmsnorm(v, g_v)
    vc = vn.reshape(B, T // L, L, SGU_GROUPS, SGU_GW)
    mask = jnp.tril(jnp.ones((L, L), bool))
    ws = jnp.where(mask[None], w_s[:, :L, :L], jnp.zeros((), w_s.dtype))
    mixed = jnp.einsum('gij,bcjgd->bcigd', ws, vc) + b_s[:, :L].T[None, None, :, :, None]
    return u * mixed.reshape(B, T, SGU_W), vn


def rel_bias(table, n_q, n_k, offset):
    d = offset + jnp.arange(n_q)[:, None] - jnp.arange(n_k)[None, :]
    idx = jnp.clip(d, -REL_CLIP, REL_CLIP) + REL_CLIP
    return table[:, idx]


def band_attention(q, k, v, bias, mask):
    s = jnp.einsum('...qhd,...khd->...hqk', q, k).astype(jnp.float32) * (HEAD_DIM ** -0.5)
    s = s + bias.astype(jnp.float32)
    if mask is not None:
        s = jnp.where(mask, s, -1e30)
    p = jax.nn.softmax(s, axis=-1).astype(v.dtype)
    return jnp.einsum('...hqk,...khd->...qhd', p, v)


def attn_prompt(q, k, v, table):
    B, T, H, hd = q.shape
    nC = T // CHUNK
    pad = jnp.zeros((B, BAND, H, hd), k.dtype)
    kc = jnp.concatenate([pad, k], axis=1).reshape(B, nC + BAND_CHUNKS, CHUNK, H, hd)
    vc = jnp.concatenate([pad, v], axis=1).reshape(B, nC + BAND_CHUNKS, CHUNK, H, hd)
    nk = BAND + CHUNK
    kb = jnp.stack([kc[:, j:j + nC] for j in range(BAND_CHUNKS + 1)], axis=2).reshape(B, nC, nk, H, hd)
    vb = jnp.stack([vc[:, j:j + nC] for j in range(BAND_CHUNKS + 1)], axis=2).reshape(B, nC, nk, H, hd)
    qc = q.reshape(B, nC, CHUNK, H, hd)
    bias = rel_bias(table, CHUNK, nk, BAND)
    key_chunk = jnp.arange(nC)[:, None] - BAND_CHUNKS + jnp.arange(nk)[None, :] // CHUNK
    mask = (key_chunk >= 0)[None, :, None, None, :]
    o = band_attention(qc, kb, vb, bias, mask)
    return o.reshape(B, T, H * hd)


def attn_sample(q, k, v, k_cache, v_cache, table):
    B, S, H, hd = q.shape
    n_cache = k_cache.shape[1]
    kk = jnp.concatenate([k_cache.astype(k.dtype), k], axis=1)
    vv = jnp.concatenate([v_cache.astype(v.dtype), v], axis=1)
    bias = rel_bias(table, S, n_cache + S, n_cache)
    o = band_attention(q, kk, vv, bias, None)
    return o.reshape(B, S, H * hd)


def peer(x, w_query, sub_keys, u_tab, v_tab):
    B, T, D = x.shape
    n = B * T
    blk = min(PEER_BLOCK, n)
    pad = (-n) % blk
    xb = jnp.pad(x.reshape(n, D), ((0, pad), (0, 0))).reshape(-1, blk, D)

    def one_block(xt):
        q = (xt @ w_query).reshape(blk, PEER_HEADS, 2, PK_HALF)
        s = jnp.einsum('nhpc,pkc->nhpk', q, sub_keys).astype(jnp.float32)
        sv, si = lax.top_k(s, PK_TOPK)
        cand = (sv[:, :, 0, :, None] + sv[:, :, 1, None, :]).reshape(blk, PEER_HEADS, PK_TOPK * PK_TOPK)
        cidx = (si[:, :, 0, :, None] * N_KEYS + si[:, :, 1, None, :]).reshape(blk, PEER_HEADS, PK_TOPK * PK_TOPK)
        top_s, pos = lax.top_k(cand, PK_TOPK)
        eidx = jnp.take_along_axis(cidx, pos, axis=-1)
        g = jax.nn.softmax(top_s, axis=-1).astype(xt.dtype)
        hid = jax.nn.gelu(jnp.einsum('nhkd,nd->nhk', u_tab[eidx], xt), approximate=False)
        return jnp.einsum('nhk,nhkd->nd', g * hid, v_tab[eidx])

    yb = lax.map(one_block, xb)
    return yb.reshape(-1, D)[:n].reshape(B, T, D)


def trunk_layer(x, p, pool_prefix, k_cache, v_cache):
    B, T, _ = x.shape
    xn = rmsnorm(x, p['g_mix'])
    z = xn @ p['w_in']
    a, u_b, v_b, q, k, v = jnp.split(z, SPLITS, axis=-1)
    y_a, pool_state = pool_mix(a, pool_prefix, p['pool_w'], p['pool_scale'])
    y_b, sgu_v = sgu_mix(u_b, v_b, p['sgu_norm'], p['sgu_w'], p['sgu_b'])
    q = rmsnorm(q.reshape(B, T, ATT_HEADS, HEAD_DIM), p['q_norm'])
    k = rmsnorm(k.reshape(B, T, ATT_HEADS, HEAD_DIM), p['k_norm'])
    v = v.reshape(B, T, ATT_HEADS, HEAD_DIM)
    if k_cache is None:
        y_c = attn_prompt(q, k, v, p['rel_table'])
        keep = min(BAND, T)
        k_state, v_state = k[:, T - keep:], v[:, T - keep:]
    else:
        y_c = attn_sample(q, k, v, k_cache, v_cache, p['rel_table'])
        k_state, v_state = k, v
    gates = jax.nn.sigmoid(xn @ p['w_gate'] + p['b_gate']).reshape(B, T, N_BRANCH, D_MODEL)
    merged = (gates[:, :, 0] * (y_a @ p['w_br_a'])
              + gates[:, :, 1] * (y_b @ p['w_br_b'])
              + gates[:, :, 2] * (y_c @ p['w_br_c']))
    h = x + merged @ p['w_out']
    out = h + peer(rmsnorm(h, p['g_ffn']), p['peer_wq'], p['peer_subkeys'], p['peer_u'], p['peer_v'])
    return out, pool_state, k_state, v_state, sgu_v


def setup_inputs(seed: int = 0) -> dict:
    key = jax.random.key(seed)
    ks = jax.random.split(key, 32)
    nrm = lambda i, shape, s: jax.random.normal(ks[i], shape, jnp.float32) * s
    D = D_MODEL
    return {
        'x_prompt': nrm(0, (BATCH, SEQ, D), 1.0),
        'x_sample': nrm(1, (DEC_BATCH, DEC_SEQ, D), 1.0),
        'state_pool': nrm(2, (DEPTH, DEC_BATCH, POOL_STATE, POOL_W), 1.0),
        'cache_k': nrm(3, (DEPTH, DEC_BATCH, CACHE_ROWS, ATT_HEADS, HEAD_DIM), 1.0),
        'cache_v': nrm(4, (DEPTH, DEC_BATCH, CACHE_ROWS, ATT_HEADS, HEAD_DIM), 1.0),
        'g_mix': 1.0 + nrm(5, (DEPTH, D), 0.05),
        'w_in': nrm(6, (DEPTH, D, IN_W), D ** -0.5),
        'pool_w': nrm(7, (DEPTH, POOL_GROUPS, POOL_GW, POOL_GW), POOL_GW ** -0.5),
        'pool_scale': 1.0 + nrm(8, (DEPTH, POOL_W), 0.1),
        'sgu_norm': 1.0 + nrm(9, (DEPTH, SGU_W), 0.05),
        'sgu_w': nrm(10, (DEPTH, SGU_GROUPS, SGU_CHUNK, SGU_CHUNK), SGU_CHUNK ** -0.5),
        'sgu_b': 1.0 + nrm(11, (DEPTH, SGU_GROUPS, SGU_CHUNK), 0.1),
        'q_norm': 1.0 + nrm(12, (DEPTH, HEAD_DIM), 0.05),
        'k_norm': 1.0 + nrm(13, (DEPTH, HEAD_DIM), 0.05),
        'rel_table': nrm(14, (DEPTH, ATT_HEADS, 2 * REL_CLIP + 1), 0.5),
        'w_gate': nrm(15, (DEPTH, D, N_BRANCH * D), D ** -0.5),
        'b_gate': nrm(16, (DEPTH, N_BRANCH * D), 0.01),
        'w_br_a': nrm(17, (DEPTH, POOL_W, D), POOL_W ** -0.5),
        'w_br_b': nrm(18, (DEPTH, SGU_W, D), SGU_W ** -0.5),
        'w_br_c': nrm(19, (DEPTH, ATT_W, D), ATT_W ** -0.5),
        'w_out': nrm(20, (DEPTH, D, D), D ** -0.5),
        'g_ffn': 1.0 + nrm(21, (DEPTH, D), 0.05),
        'peer_wq': nrm(22, (DEPTH, D, PEER_HEADS * PK_DIM), D ** -0.5),
        'peer_subkeys': nrm(23, (DEPTH, 2, N_KEYS, PK_HALF), PK_HALF ** -0.5),
        'peer_u': nrm(24, (DEPTH, N_EXPERTS, D), D ** -0.5),
        'peer_v': nrm(25, (DEPTH, N_EXPERTS, D), PEER_HEADS ** -0.5),
    }


def reference(x_prompt, x_sample, state_pool, cache_k, cache_v, g_mix, w_in, pool_w, pool_scale,
              sgu_norm, sgu_w, sgu_b, q_norm, k_norm, rel_table, w_gate, b_gate, w_br_a, w_br_b,
              w_br_c, w_out, g_ffn, peer_wq, peer_subkeys, peer_u, peer_v):
    xp, xs = x_prompt, x_sample
    pool_p, pool_s, k_p, v_p, k_s, v_s, sgu_s = [], [], [], [], [], [], []
    for l in range(DEPTH):
        p = {
            'g_mix': g_mix[l], 'w_in': w_in[l], 'pool_w': pool_w[l], 'pool_scale': pool_scale[l],
            'sgu_norm': sgu_norm[l], 'sgu_w': sgu_w[l], 'sgu_b': sgu_b[l],
            'q_norm': q_norm[l], 'k_norm': k_norm[l], 'rel_table': rel_table[l],
            'w_gate': w_gate[l], 'b_gate': b_gate[l], 'w_br_a': w_br_a[l], 'w_br_b': w_br_b[l],
            'w_br_c': w_br_c[l], 'w_out': w_out[l], 'g_ffn': g_ffn[l],
            'peer_wq': peer_wq[l], 'peer_subkeys': peer_subkeys[l], 'peer_u': peer_u[l], 'peer_v': peer_v[l],
        }
        zero_prefix = jnp.zeros((xp.shape[0], POOL_STATE, POOL_W), xp.dtype)
        xp, sp_l, kp_l, vp_l, _ = trunk_layer(xp, p, zero_prefix, None, None)
        xs, ss_l, ks_l, vs_l, sg_l = trunk_layer(xs, p, state_pool[l], cache_k[l], cache_v[l])
        pool_p.append(sp_l); pool_s.append(ss_l)
        k_p.append(kp_l); v_p.append(vp_l)
        k_s.append(ks_l); v_s.append(vs_l)
        sgu_s.append(sg_l)
    return (xp, xs, jnp.stack(pool_p), jnp.stack(pool_s), jnp.stack(k_p), jnp.stack(v_p),
            jnp.stack(k_s), jnp.stack(v_s), jnp.stack(sgu_s))
```

```python
import functools

import jax
import jax.numpy as jnp
from jax import lax
from jax.experimental import pallas as pl
from jax.experimental.pallas import tpu as pltpu

F32 = jnp.float32
BF16 = jnp.bfloat16

EPS = 1e-6
D_MODEL = 1024
POOL_W = 256
POOL_GW = 64
POOL_STATE = 15
SGU_W = 256
SGU_GW = 64
SGU_CHUNK = 128
ATT_HEADS = 8
HEAD_DIM = 64
ATT_W = ATT_HEADS * HEAD_DIM
CHUNK = 64
BAND = 512
REL_CLIP = 128
N_BRANCH = 3
PEER_HEADS = 8
N_KEYS = 128
PK_HALF = 128
PK_TOPK = 16
NEG_BIG = -1e30

LANES = 128
SUBLANES = 8
ROW_TILE = 256
ATT_QB = 256
HALO = 16
TOPK_TT = 128
PEER_EC = 1024
PEER_ROWS_PER_CHUNK = PEER_EC // N_KEYS
PEER_NC = N_KEYS * N_KEYS // PEER_EC
VMEM_LIMIT = 56 * 1024 * 1024


def _cparams(sem):
    return pltpu.CompilerParams(dimension_semantics=sem, vmem_limit_bytes=VMEM_LIMIT)


def _const_spec(shape):
    n = len(shape)
    return pl.BlockSpec(shape, lambda *_: (0,) * n)


def _nt_dot(a, b):
    return lax.dot_general(a, b, (((1,), (1,)), ((), ())), preferred_element_type=F32)


def _split_bf16(x):
    hi = x.astype(BF16)
    lo = (x - hi.astype(F32)).astype(BF16)
    return hi, lo


def _in_proj_kernel(x_ref, gmix_ref, win_ref, sgn_ref, qn_ref, kn_ref, hsum_ref, ms_ref, bs_ref,
                    a_ref, yb_ref, vn_ref, q_ref, k_ref, v_ref):
    x = x_ref[...]
    xn = x * lax.rsqrt(jnp.mean(x * x, axis=-1, keepdims=True) + EPS) * gmix_ref[...]
    z = jnp.dot(xn.astype(BF16), win_ref[...], preferred_element_type=F32)
    a_ref[...] = z[:, 0:POOL_W]
    u = z[:, POOL_W:POOL_W + SGU_W]
    vb = z[:, POOL_W + SGU_W:POOL_W + 2 * SGU_W]
    vn = vb * lax.rsqrt(jnp.mean(vb * vb, axis=-1, keepdims=True) + EPS) * sgn_ref[...]
    vn_ref[...] = vn
    vnb = vn.astype(BF16)
    lane_group = lax.broadcasted_iota(jnp.int32, (SGU_CHUNK, SGU_W), 1) // SGU_GW
    rows = x.shape[0]
    for c in range(rows // SGU_CHUNK):
        sl = slice(c * SGU_CHUNK, (c + 1) * SGU_CHUNK)
        vc = vnb[sl, :]
        mixed = jnp.dot(ms_ref[0, 0], vc, preferred_element_type=F32)
        for g in range(1, SGU_W // SGU_GW):
            mg = jnp.dot(ms_ref[0, g], vc, preferred_element_type=F32)
            mixed = jnp.where(lane_group == g, mg, mixed)
        yb_ref[sl, :] = (u[sl, :] * (mixed + bs_ref[0])).astype(BF16)

    base = POOL_W + 2 * SGU_W
    q = z[:, base:base + ATT_W]
    k = z[:, base + ATT_W:base + 2 * ATT_W]
    v_ref[...] = z[:, base + 2 * ATT_W:base + 3 * ATT_W]

    def head_norm(t, w_ref):
        hi, lo = _split_bf16(t * t)
        m = (jnp.dot(hi, hsum_ref[...], preferred_element_type=F32)
             + jnp.dot(lo, hsum_ref[...], preferred_element_type=F32))
        return t * lax.rsqrt(m + EPS) * w_ref[...]

    q_ref[...] = (head_norm(q, qn_ref) * (HEAD_DIM ** -0.5)).astype(BF16)
    k_ref[...] = head_norm(k, kn_ref)


def _in_proj(x, gmix, win, sgn, qn, kn, hsum, ms, bs, n_prompt_tiles):
    n = x.shape[0]
    tm = ROW_TILE
    in_w = win.shape[1]
    row = lambda w: pl.BlockSpec((tm, w), lambda i: (i, 0))
    sel = lambda i: (jnp.where(i >= n_prompt_tiles, 1, 0), 0, 0, 0)
    sel3 = lambda i: (jnp.where(i >= n_prompt_tiles, 1, 0), 0, 0)
    return pl.pallas_call(
        _in_proj_kernel,
        grid=(n // tm,),
        in_specs=[row(D_MODEL), _const_spec((1, D_MODEL)), _const_spec((D_MODEL, in_w)),
                  _const_spec((1, SGU_W)), _const_spec((1, ATT_W)), _const_spec((1, ATT_W)),
                  _const_spec((ATT_W, ATT_W)),
                  pl.BlockSpec((1, SGU_W // SGU_GW, SGU_CHUNK, SGU_CHUNK), sel),
                  pl.BlockSpec((1, SGU_CHUNK, SGU_W), sel3)],
        out_specs=[row(POOL_W), row(SGU_W), row(SGU_W), row(ATT_W), row(ATT_W), row(ATT_W)],
        out_shape=[jax.ShapeDtypeStruct((n, POOL_W), F32), jax.ShapeDtypeStruct((n, SGU_W), BF16),
                   jax.ShapeDtypeStruct((n, SGU_W), F32), jax.ShapeDtypeStruct((n, ATT_W), BF16),
                   jax.ShapeDtypeStruct((n, ATT_W), F32), jax.ShapeDtypeStruct((n, ATT_W), F32)],
        compiler_params=_cparams(("arbitrary",)),
        name="in_proj",
    )(x, gmix, win, sgn, qn, kn, hsum, ms, bs)


def _pool_kernel(a_ref, prev_ref, pre_ref, pw_ref, sc_ref, y_ref):
    t = pl.program_id(1)
    a = a_ref[...]
    tm = a.shape[0]
    halo = jnp.where(t == 0, pre_ref[0], prev_ref[...])
    e = jnp.concatenate([halo, a], axis=0)
    s2 = e[1:] + e[:-1]
    s4 = s2[2:] + s2[:-2]
    s8 = s4[4:] + s4[:-4]
    s16 = s8[8:] + s8[:-8]
    lg = lax.broadcasted_iota(jnp.int32, (tm, POOL_W), 1) // POOL_GW
    mean = jnp.where(lg == 0, s2[15:] * 0.5,
                     jnp.where(lg == 1, s4[13:] * 0.25,
                               jnp.where(lg == 2, s8[9:] * 0.125, s16[1:] * 0.0625)))
    pooled = mean - a
    y = jnp.dot(pooled.astype(BF16), pw_ref[...], preferred_element_type=F32) * sc_ref[...]
    y_ref[...] = y.astype(BF16)


def _pool(a, prefix, pw, sc, row0, n_streams, t_len, tm):
    nt = t_len // tm
    b0 = row0 // tm
    hb = tm // HALO
    return pl.pallas_call(
        _pool_kernel,
        grid=(n_streams, nt),
        in_specs=[pl.BlockSpec((tm, POOL_W), lambda b, t: (b0 + b * nt + t, 0)),
                  pl.BlockSpec((HALO, POOL_W), lambda b, t: (jnp.maximum((b0 + b * nt + t) * hb - 1, 0), 0)),
                  pl.BlockSpec((1, HALO, POOL_W), lambda b, t: (b, 0, 0)),
                  _const_spec((POOL_W, POOL_W)), _const_spec((1, POOL_W))],
        out_specs=pl.BlockSpec((tm, POOL_W), lambda b, t: (b * nt + t, 0)),
        out_shape=jax.ShapeDtypeStruct((n_streams * t_len, POOL_W), BF16),
        compiler_params=_cparams(("arbitrary", "arbitrary")),
        name="pool_mix",
    )(a, a, prefix, pw, sc)


def _attend(q, k, v, bias_ref, col_bias, o_ref):
    qb = q.shape[0]
    lane_hi = lax.broadcasted_iota(jnp.int32, (qb, 2 * HEAD_DIM), 1) >= HEAD_DIM
    for hp in range(ATT_HEADS // 2):
        sl = slice(hp * 2 * HEAD_DIM, (hp + 1) * 2 * HEAD_DIM)
        q2, k2, v2 = q[:, sl], k[:, sl], v[:, sl]
        out = None
        for sub in range(2):
            qm = jnp.where(lane_hi == (sub == 1), q2, jnp.zeros_like(q2))
            s = _nt_dot(qm, k2) + bias_ref[hp * 2 + sub]
            if col_bias is not None:
                s = s + col_bias
            m = jnp.max(s, axis=-1, keepdims=True)
            p = jnp.exp(s - m)
            l = jnp.sum(p, axis=-1, keepdims=True)
            o = jnp.dot(p.astype(BF16), v2, preferred_element_type=F32) / l
            out = o if sub == 0 else jnp.where(lane_hi, o, out)
        o_ref[:, sl] = out.astype(BF16)


def _attn_prompt_kernel(q_ref, k0_ref, k1_ref, k2_ref, v0_ref, v1_ref, v2_ref, bias_ref, o_ref):
    t = pl.program_id(1)
    qb = q_ref.shape[0]
    k = jnp.concatenate([k0_ref[...], k1_ref[...], k2_ref[...]], axis=0).astype(BF16)
    v = jnp.concatenate([v0_ref[...], v1_ref[...], v2_ref[...]], axis=0).astype(BF16)
    key_row = (t - 2) * qb + lax.broadcasted_iota(jnp.int32, (1, 3 * qb), 1)
    col_bias = jnp.where(key_row >= 0, 0.0, NEG_BIG).astype(F32)
    _attend(q_ref[...], k, v, bias_ref, col_bias, o_ref)


def _attn_prompt(q, k, v, bias, n_streams, t_len):
    qb = ATT_QB
    nt = t_len // qb
    cur = lambda b, t: (b * nt + t, 0)
    prev1 = lambda b, t: (b * nt + jnp.maximum(t - 1, 0), 0)
    prev2 = lambda b, t: (b * nt + jnp.maximum(t - 2, 0), 0)
    blk = lambda im: pl.BlockSpec((qb, ATT_W), im)
    return pl.pallas_call(
        _attn_prompt_kernel,
        grid=(n_streams, nt),
        in_specs=[blk(cur), blk(prev2), blk(prev1), blk(cur), blk(prev2), blk(prev1), blk(cur),
                  _const_spec(bias.shape)],
        out_specs=blk(cur),
        out_shape=jax.ShapeDtypeStruct((n_streams * t_len, ATT_W), BF16),
        compiler_params=_cparams(("arbitrary", "arbitrary")),
        name="attn_prompt",
    )(q, k, k, k, v, v, v, bias)


def _attn_sample_kernel(q_ref, kn_ref, vn_ref, kc_ref, vc_ref, bias_ref, o_ref):
    k = jnp.concatenate([kc_ref[0], kn_ref[...]], axis=0).astype(BF16)
    v = jnp.concatenate([vc_ref[0], vn_ref[...]], axis=0).astype(BF16)
    _attend(q_ref[...], k, v, bias_ref, None, o_ref)


def _attn_sample(q, k, v, kc, vc, bias, row0, n_streams, s_len):
    b0 = row0 // s_len
    n_cache = kc.shape[1]
    new = pl.BlockSpec((s_len, ATT_W), lambda s: (b0 + s, 0))
    cache = pl.BlockSpec((1, n_cache, ATT_W), lambda s: (s, 0, 0))
    return pl.pallas_call(
        _attn_sample_kernel,
        grid=(n_streams,),
        in_specs=[new, new, new, cache, cache, _const_spec(bias.shape)],
        out_specs=pl.BlockSpec((s_len, ATT_W), lambda s: (s, 0)),
        out_shape=jax.ShapeDtypeStruct((n_streams * s_len, ATT_W), BF16),
        compiler_params=_cparams(("arbitrary",)),
        name="attn_sample",
    )(q, k, v, kc, vc, bias)


def _merge_kernel(x_ref, ya_ref, yb_ref, yc_ref, gmix_ref, wg_ref, bg_ref, wa_ref, wb_ref, wc_ref,
                  wo_ref, gffn_ref, wq_ref, sk_ref, h_ref, hn_ref, st_ref):
    x = x_ref[...]
    xn = (x * lax.rsqrt(jnp.mean(x * x, axis=-1, keepdims=True) + EPS) * gmix_ref[...]).astype(BF16)
    merged = None
    for b, (y_ref, w_ref) in enumerate(((ya_ref, wa_ref), (yb_ref, wb_ref), (yc_ref, wc_ref))):
        cols = slice(b * D_MODEL, (b + 1) * D_MODEL)
        gate = jax.nn.sigmoid(jnp.dot(xn, wg_ref[:, cols], preferred_element_type=F32) + bg_ref[:, cols])
        term = gate * jnp.dot(y_ref[...], w_ref[...], preferred_element_type=F32)
        merged = term if merged is None else merged + term
    h = x + jnp.dot(merged.astype(BF16), wo_ref[...], preferred_element_type=F32)
    h_ref[...] = h
    hn = (h * lax.rsqrt(jnp.mean(h * h, axis=-1, keepdims=True) + EPS) * gffn_ref[...]).astype(BF16)
    hn_ref[...] = hn
    qp = jnp.dot(hn, wq_ref[...], preferred_element_type=F32).astype(BF16)
    for hp in range(2 * PEER_HEADS):
        st_ref[hp] = _nt_dot(sk_ref[hp % 2], qp[:, hp * PK_HALF:(hp + 1) * PK_HALF])


def _merge(x, ya, yb, yc, gmix, wg, bg, wa, wb, wc, wo, gffn, wq, sk):
    n = x.shape[0]
    tm = ROW_TILE
    row = lambda w: pl.BlockSpec((tm, w), lambda i: (i, 0))
    nq = wq.shape[1]
    return pl.pallas_call(
        _merge_kernel,
        grid=(n // tm,),
        in_specs=[row(D_MODEL), row(POOL_W), row(SGU_W), row(ATT_W),
                  _const_spec((1, D_MODEL)), _const_spec(wg.shape), _const_spec(bg.shape),
                  _const_spec(wa.shape), _const_spec(wb.shape), _const_spec(wc.shape),
                  _const_spec(wo.shape), _const_spec((1, D_MODEL)), _const_spec(wq.shape),
                  _const_spec(sk.shape)],
        out_specs=[row(D_MODEL), row(D_MODEL),
                   pl.BlockSpec((2 * PEER_HEADS, N_KEYS, tm), lambda i: (0, 0, i))],
        out_shape=[jax.ShapeDtypeStruct((n, D_MODEL), F32), jax.ShapeDtypeStruct((n, D_MODEL), BF16),
                   jax.ShapeDtypeStruct((2 * PEER_HEADS, N_KEYS, n), F32)],
        compiler_params=_cparams(("arbitrary",)),
        name="merge",
    )(x, ya, yb, yc, gmix, wg, bg, wa, wb, wc, wo, gffn, wq, sk)


def _top16(x, iota, n):
    vals, idxs = [], []
    for _ in range(PK_TOPK):
        m = jnp.max(x, axis=0, keepdims=True)
        idx = jnp.min(jnp.where(x == m, iota, n), axis=0, keepdims=True)
        vals.append(m)
        idxs.append(idx)
        x = jnp.where(iota == idx, -jnp.inf, x)
    return vals, idxs


def _topk_kernel(st_ref, i_ref, j_ref, g_ref):
    tt = st_ref.shape[2]
    iota_k = lax.broadcasted_iota(jnp.int32, (N_KEYS, tt), 0)
    iota_c = lax.broadcasted_iota(jnp.int32, (PK_TOPK * PK_TOPK, tt), 0)
    iota_16 = lax.broadcasted_iota(jnp.int32, (PK_TOPK, tt), 0)
    i_rows, j_rows, g_rows = [], [], []
    for h in range(PEER_HEADS):
        v0, i0 = _top16(st_ref[2 * h], iota_k, N_KEYS)
        v1, i1 = _top16(st_ref[2 * h + 1], iota_k, N_KEYS)
        sv1 = jnp.concatenate(v1, axis=0)
        si0 = jnp.concatenate(i0, axis=0)
        si1 = jnp.concatenate(i1, axis=0)
        cand = jnp.concatenate([v0[a] + sv1 for a in range(PK_TOPK)], axis=0)
        tv, tf = _top16(cand, iota_c, PK_TOPK * PK_TOPK)
        ts = jnp.concatenate(tv, axis=0)
        e = jnp.exp(ts - tv[0])
        g_rows.append(e / jnp.sum(e, axis=0, keepdims=True))
        for r in range(PK_TOPK):
            a_sel = iota_16 == (tf[r] >> 4)
            b_sel = iota_16 == (tf[r] & (PK_TOPK - 1))
            i_rows.append(jnp.sum(jnp.where(a_sel, si0, 0), axis=0, keepdims=True))
            j_rows.append(jnp.sum(jnp.where(b_sel, si1, 0), axis=0, keepdims=True))
    i_ref[...] = jnp.concatenate(i_rows, axis=0).astype(F32).T
    j_ref[...] = jnp.concatenate(j_rows, axis=0).astype(F32).T
    g_ref[...] = jnp.concatenate(g_rows, axis=0).T


def _topk(st):
    n = st.shape[2]
    tt = TOPK_TT
    nsel = PEER_HEADS * PK_TOPK
    out = pl.BlockSpec((tt, nsel), lambda i: (i, 0))
    return pl.pallas_call(
        _topk_kernel,
        grid=(n // tt,),
        in_specs=[pl.BlockSpec((2 * PEER_HEADS, N_KEYS, tt), lambda i: (0, 0, i))],
        out_specs=[out, out, out],
        out_shape=[jax.ShapeDtypeStruct((n, nsel), F32)] * 3,
        compiler_params=_cparams(("arbitrary",)),
        name="peer_topk",
    )(st)


def _gelu(x):
    return 0.5 * x * (1.0 + lax.erf(x * (2.0 ** -0.5)))


def _peer_kernel(hn_ref, h_ref, i_ref, j_ref, g_ref, u_ref, v_ref, o_ref, hs_ref, acc_ref):
    s = pl.program_id(1)
    tb = hn_ref.shape[0]
    rpc = PEER_ROWS_PER_CHUNK

    @pl.when(s < PEER_NC)
    def _scores():
        hc = _nt_dot(hn_ref[...], u_ref[...])
        slab = hs_ref.at[s]
        for ib in range(rpc):
            slab[pl.ds(ib, tb, stride=rpc), :] = hc[:, ib * N_KEYS:(ib + 1) * N_KEYS]

    @pl.when(s == PEER_NC - 1)
    def _weights():
        sub = lax.broadcasted_iota(jnp.int32, (N_KEYS, N_KEYS), 0).astype(F32)

        def body(n, carry):
            irow = i_ref[pl.ds(n, 1), :]
            jrow = j_ref[pl.ds(n, 1), :]
            grow = g_ref[pl.ds(n, 1), :]
            at = jnp.where(sub == irow, 1.0, 0.0).astype(BF16)
            g_hi, g_lo = _split_bf16(jnp.where(sub == jrow, grow, 0.0))
            gmat = _nt_dot(jnp.concatenate([at, at], axis=1),
                           jnp.concatenate([g_hi, g_lo], axis=1))
            r0 = pl.multiple_of(n * rpc, rpc)
            for c in range(PEER_NC):
                hv = hs_ref[c, pl.ds(r0, rpc), :]
                hs_ref[c, pl.ds(r0, rpc), :] = _gelu(hv) * gmat[c * rpc:(c + 1) * rpc, :]
            return carry

        lax.fori_loop(0, tb, body, 0)

    @pl.when(s >= PEER_NC)
    def _combine():
        c = s - PEER_NC
        slab = hs_ref.at[c]
        wc = jnp.concatenate([slab[pl.ds(ib, tb, stride=rpc), :] for ib in range(rpc)], axis=1)
        part = jnp.dot(wc.astype(BF16), v_ref[...], preferred_element_type=F32)

        @pl.when(c == 0)
        def _():
            acc_ref[...] = part

        @pl.when(c > 0)
        def _():
            acc_ref[...] += part

    @pl.when(s == 2 * PEER_NC - 1)
    def _finish():
        o_ref[...] = h_ref[...] + acc_ref[...]


def _peer(hn, h, isel, jsel, gsel, u, v, tb):
    n = hn.shape[0]
    nsel = PEER_HEADS * PK_TOPK
    nc = PEER_NC
    row = lambda w: pl.BlockSpec((tb, w), lambda b, s: (b, 0))
    return pl.pallas_call(
        _peer_kernel,
        grid=(n // tb, 2 * nc),
        in_specs=[row(D_MODEL), row(D_MODEL), row(nsel), row(nsel), row(nsel),
                  pl.BlockSpec((PEER_EC, D_MODEL), lambda b, s: (jnp.minimum(s, nc - 1), 0)),
                  pl.BlockSpec((PEER_EC, D_MODEL), lambda b, s: (jnp.maximum(s - nc, 0), 0))],
        out_specs=row(D_MODEL),
        out_shape=jax.ShapeDtypeStruct((n, D_MODEL), F32),
        scratch_shapes=[pltpu.VMEM((nc, tb * PEER_ROWS_PER_CHUNK, N_KEYS), F32),
                        pltpu.VMEM((tb, D_MODEL), F32)],
        compiler_params=_cparams(("arbitrary", "arbitrary")),
        name="peer_experts",
    )(hn, h, isel, jsel, gsel, u, v)


def _rel_bias(table, n_q, n_k, offset):
    d = offset + jnp.arange(n_q)[:, None] - jnp.arange(n_k)[None, :]
    return table[:, jnp.clip(d, -REL_CLIP, REL_CLIP) + REL_CLIP]


def _prompt_bias(table):
    nk = 3 * ATT_QB
    bias = _rel_bias(table, ATT_QB, nk, nk - ATT_QB)
    qc = jnp.arange(ATT_QB)[:, None] // CHUNK
    kc = jnp.arange(nk)[None, :] // CHUNK
    band = (kc - qc >= 0) & (kc - qc <= BAND // CHUNK)
    return jnp.where(band[None], bias, NEG_BIG).astype(F32)


def _block_diag(blocks):
    g, r, c = blocks.shape
    eye = jnp.eye(g, dtype=blocks.dtype)
    return (eye[:, None, :, None] * blocks[:, :, None, :]).reshape(g * r, g * c)


def _sgu_mats(w_s, b_s, length):
    g = w_s.shape[0]
    reps = SGU_CHUNK // length
    tri = jnp.tril(jnp.ones((length, length), bool))
    ws = jnp.where(tri[None], w_s[:, :length, :length], 0.0)
    eye = jnp.eye(reps, dtype=ws.dtype)
    mats = (eye[None, :, None, :, None] * ws[:, None, :, None, :]).reshape(g, SGU_CHUNK, SGU_CHUNK)
    bias = jnp.tile(b_s[:, :length], (1, reps))
    bias = jnp.repeat(bias.T, SGU_GW, axis=1)
    return mats, bias


def kernel(x_prompt, x_sample, state_pool, cache_k, cache_v, g_mix, w_in, pool_w, pool_scale, sgu_norm, sgu_w, sgu_b, q_norm, k_norm, rel_table, w_gate, b_gate, w_br_a, w_br_b, w_br_c, w_out, g_ffn, peer_wq, peer_subkeys, peer_u, peer_v):
    depth = g_mix.shape[0]
    bsz, seq, d = x_prompt.shape
    dbsz, dseq, _ = x_sample.shape
    n_p, n_s = bsz * seq, dbsz * dseq
    n_cache = cache_k.shape[2]
    assert seq % ATT_QB == 0 and n_p % ROW_TILE == 0 and n_s % ROW_TILE == 0
    assert dseq <= SGU_CHUNK and SGU_CHUNK % dseq == 0 and dseq >= POOL_STATE + 1 and dseq % SUBLANES == 0
    x = jnp.concatenate([x_prompt.reshape(n_p, d), x_sample.reshape(n_s, d)], axis=0)

    hsum = _block_diag(jnp.full((ATT_HEADS, HEAD_DIM, HEAD_DIM), 1.0 / HEAD_DIM, F32)).astype(BF16)
    zero_prefix = jnp.zeros((bsz, HALO, POOL_W), F32)
    keep = min(BAND, seq)
    outs = {name: [] for name in ("pool_p", "pool_s", "k_p", "v_p", "k_s", "v_s", "sgu_s")}

    for l in range(depth):
        row = lambda a: a[l].reshape(1, -1)
        m_p, b_p = _sgu_mats(sgu_w[l], sgu_b[l], SGU_CHUNK)
        m_s, b_s = _sgu_mats(sgu_w[l], sgu_b[l], dseq)
        ms = jnp.stack([m_p, m_s]).astype(BF16)
        bs = jnp.stack([b_p, b_s])
        a, yb, vn, q, k, v = _in_proj(
            x, row(g_mix), w_in[l].astype(BF16), row(sgu_norm),
            jnp.tile(q_norm[l], ATT_HEADS).reshape(1, -1), jnp.tile(k_norm[l], ATT_HEADS).reshape(1, -1),
            hsum, ms, bs, n_p // ROW_TILE)

        pw = _block_diag(pool_w[l]).astype(BF16)
        sample_prefix = jnp.pad(state_pool[l], ((0, 0), (HALO - POOL_STATE, 0), (0, 0)))
        ya_p = _pool(a, zero_prefix, pw, row(pool_scale), 0, bsz, seq, ROW_TILE)
        ya_s = _pool(a, sample_prefix, pw, row(pool_scale), n_p, dbsz, dseq, dseq)

        yc_p = _attn_prompt(q, k, v, _prompt_bias(rel_table[l]), bsz, seq)
        yc_s = _attn_sample(q, k, v, cache_k[l].reshape(dbsz, n_cache, ATT_W),
                            cache_v[l].reshape(dbsz, n_cache, ATT_W),
                            _rel_bias(rel_table[l], dseq, n_cache + dseq, n_cache).astype(F32),
                            n_p, dbsz, dseq)

        h, hn, st = _merge(
            x, jnp.concatenate([ya_p, ya_s], axis=0), yb, jnp.concatenate([yc_p, yc_s], axis=0),
            row(g_mix), w_gate[l].astype(BF16), row(b_gate), w_br_a[l].astype(BF16),
            w_br_b[l].astype(BF16), w_br_c[l].astype(BF16), w_out[l].astype(BF16), row(g_ffn),
            peer_wq[l].astype(BF16), peer_subkeys[l].astype(BF16))
        isel, jsel, gsel = _topk(st)
        x = _peer(hn, h, isel, jsel, gsel, peer_u[l].astype(BF16), peer_v[l].astype(BF16), ROW_TILE)

        a_p = a[:n_p].reshape(bsz, seq, POOL_W)
        a_s = a[n_p:].reshape(dbsz, dseq, POOL_W)
        outs["pool_p"].append(a_p[:, seq - POOL_STATE:])
        outs["pool_s"].append(a_s[:, dseq - POOL_STATE:])
        k_p = k[:n_p].reshape(bsz, seq, ATT_HEADS, HEAD_DIM)
        v_p = v[:n_p].reshape(bsz, seq, ATT_HEADS, HEAD_DIM)
        outs["k_p"].append(k_p[:, seq - keep:])
        outs["v_p"].append(v_p[:, seq - keep:])
        outs["k_s"].append(k[n_p:].reshape(dbsz, dseq, ATT_HEADS, HEAD_DIM))
        outs["v_s"].append(v[n_p:].reshape(dbsz, dseq, ATT_HEADS, HEAD_DIM))
        outs["sgu_s"].append(vn[n_p:].reshape(dbsz, dseq, SGU_W))

    st = lambda name: jnp.stack(outs[name])
    return (x[:n_p].reshape(bsz, seq, d), x[n_p:].reshape(dbsz, dseq, d), st("pool_p"), st("pool_s"),
            st("k_p"), st("v_p"), st("k_s"), st("v_s"), st("sgu_s"))
```

```python
import functools

import jax
import jax.numpy as jnp
from jax import lax
from jax.experimental import pallas as pl
from jax.experimental.pallas import tpu as pltpu

F32 = jnp.float32
BF16 = jnp.bfloat16

EPS = 1e-6
D_MODEL = 1024
POOL_W = 256
POOL_GW = 64
POOL_STATE = 15
SGU_W = 256
SGU_GW = 64
SGU_CHUNK = 128
ATT_HEADS = 8
HEAD_DIM = 64
ATT_W = ATT_HEADS * HEAD_DIM
CHUNK = 64
BAND = 512
REL_CLIP = 128
N_BRANCH = 3
PEER_HEADS = 8
N_KEYS = 128
PK_HALF = 128
PK_TOPK = 16
NEG_BIG = -1e30

LANES = 128
SUBLANES = 8
ROW_TILE = 256
ATT_QB = 256
HALO = 16
TOPK_TT = 128
PEER_EC = 1024
PEER_ROWS_PER_CHUNK = PEER_EC // N_KEYS
PEER_NC = N_KEYS * N_KEYS // PEER_EC
PEER_TOKEN_UNROLL = 8
VMEM_LIMIT = 56 * 1024 * 1024


def _cparams(sem):
    return pltpu.CompilerParams(dimension_semantics=sem, vmem_limit_bytes=VMEM_LIMIT)


def _const_spec(shape):
    n = len(shape)
    return pl.BlockSpec(shape, lambda *_: (0,) * n)


def _nt_dot(a, b):
    return lax.dot_general(a, b, (((1,), (1,)), ((), ())), preferred_element_type=F32)


def _split_bf16(x):
    hi = x.astype(BF16)
    lo = (x - hi.astype(F32)).astype(BF16)
    return hi, lo


def _in_proj_kernel(x_ref, gmix_ref, win_ref, sgn_ref, qn_ref, kn_ref, hsum_ref, ms_ref, bs_ref,
                    a_ref, yb_ref, vn_ref, q_ref, k_ref, v_ref):
    x = x_ref[...]
    xn = x * lax.rsqrt(jnp.mean(x * x, axis=-1, keepdims=True) + EPS) * gmix_ref[...]
    z = jnp.dot(xn.astype(BF16), win_ref[...], preferred_element_type=F32)
    a_ref[...] = z[:, 0:POOL_W]
    u = z[:, POOL_W:POOL_W + SGU_W]
    vb = z[:, POOL_W + SGU_W:POOL_W + 2 * SGU_W]
    vn = vb * lax.rsqrt(jnp.mean(vb * vb, axis=-1, keepdims=True) + EPS) * sgn_ref[...]
    vn_ref[...] = vn
    vnb = vn.astype(BF16)
    lane_group = lax.broadcasted_iota(jnp.int32, (SGU_CHUNK, SGU_W), 1) // SGU_GW
    rows = x.shape[0]
    for c in range(rows // SGU_CHUNK):
        sl = slice(c * SGU_CHUNK, (c + 1) * SGU_CHUNK)
        vc = vnb[sl, :]
        mixed = jnp.dot(ms_ref[0, 0], vc, preferred_element_type=F32)
        for g in range(1, SGU_W // SGU_GW):
            mg = jnp.dot(ms_ref[0, g], vc, preferred_element_type=F32)
            mixed = jnp.where(lane_group == g, mg, mixed)
        yb_ref[sl, :] = (u[sl, :] * (mixed + bs_ref[0])).astype(BF16)

    base = POOL_W + 2 * SGU_W
    q = z[:, base:base + ATT_W]
    k = z[:, base + ATT_W:base + 2 * ATT_W]
    v_ref[...] = z[:, base + 2 * ATT_W:base + 3 * ATT_W]

    def head_norm(t, w_ref):
        hi, lo = _split_bf16(t * t)
        m = (jnp.dot(hi, hsum_ref[...], preferred_element_type=F32)
             + jnp.dot(lo, hsum_ref[...], preferred_element_type=F32))
        return t * lax.rsqrt(m + EPS) * w_ref[...]

    q_ref[...] = (head_norm(q, qn_ref) * (HEAD_DIM ** -0.5)).astype(BF16)
    k_ref[...] = head_norm(k, kn_ref)


def _in_proj(x, gmix, win, sgn, qn, kn, hsum, ms, bs, n_prompt_tiles):
    n = x.shape[0]
    tm = ROW_TILE
    in_w = win.shape[1]
    row = lambda w: pl.BlockSpec((tm, w), lambda i: (i, 0))
    sel = lambda i: (jnp.where(i >= n_prompt_tiles, 1, 0), 0, 0, 0)
    sel3 = lambda i: (jnp.where(i >= n_prompt_tiles, 1, 0), 0, 0)
    return pl.pallas_call(
        _in_proj_kernel,
        grid=(n // tm,),
        in_specs=[row(D_MODEL), _const_spec((1, D_MODEL)), _const_spec((D_MODEL, in_w)),
                  _const_spec((1, SGU_W)), _const_spec((1, ATT_W)), _const_spec((1, ATT_W)),
                  _const_spec((ATT_W, ATT_W)),
                  pl.BlockSpec((1, SGU_W // SGU_GW, SGU_CHUNK, SGU_CHUNK), sel),
                  pl.BlockSpec((1, SGU_CHUNK, SGU_W), sel3)],
        out_specs=[row(POOL_W), row(SGU_W), row(SGU_W), row(ATT_W), row(ATT_W), row(ATT_W)],
        out_shape=[jax.ShapeDtypeStruct((n, POOL_W), F32), jax.ShapeDtypeStruct((n, SGU_W), BF16),
                   jax.ShapeDtypeStruct((n, SGU_W), F32), jax.ShapeDtypeStruct((n, ATT_W), BF16),
                   jax.ShapeDtypeStruct((n, ATT_W), F32), jax.ShapeDtypeStruct((n, ATT_W), F32)],
        compiler_params=_cparams(("arbitrary",)),
        name="in_proj",
    )(x, gmix, win, sgn, qn, kn, hsum, ms, bs)


def _pool_kernel(a_ref, prev_ref, pre_ref, pw_ref, sc_ref, y_ref):
    t = pl.program_id(1)
    a = a_ref[...]
    tm = a.shape[0]
    halo = jnp.where(t == 0, pre_ref[0], prev_ref[...])
    e = jnp.concatenate([halo, a], axis=0)
    s2 = e[1:] + e[:-1]
    s4 = s2[2:] + s2[:-2]
    s8 = s4[4:] + s4[:-4]
    s16 = s8[8:] + s8[:-8]
    lg = lax.broadcasted_iota(jnp.int32, (tm, POOL_W), 1) // POOL_GW
    mean = jnp.where(lg == 0, s2[15:] * 0.5,
                     jnp.where(lg == 1, s4[13:] * 0.25,
                               jnp.where(lg == 2, s8[9:] * 0.125, s16[1:] * 0.0625)))
    pooled = mean - a
    y = jnp.dot(pooled.astype(BF16), pw_ref[...], preferred_element_type=F32) * sc_ref[...]
    y_ref[...] = y.astype(BF16)


def _pool(a, prefix, pw, sc, row0, n_streams, t_len, tm):
    nt = t_len // tm
    b0 = row0 // tm
    hb = tm // HALO
    return pl.pallas_call(
        _pool_kernel,
        grid=(n_streams, nt),
        in_specs=[pl.BlockSpec((tm, POOL_W), lambda b, t: (b0 + b * nt + t, 0)),
                  pl.BlockSpec((HALO, POOL_W), lambda b, t: (jnp.maximum((b0 + b * nt + t) * hb - 1, 0), 0)),
                  pl.BlockSpec((1, HALO, POOL_W), lambda b, t: (b, 0, 0)),
                  _const_spec((POOL_W, POOL_W)), _const_spec((1, POOL_W))],
        out_specs=pl.BlockSpec((tm, POOL_W), lambda b, t: (b * nt + t, 0)),
        out_shape=jax.ShapeDtypeStruct((n_streams * t_len, POOL_W), BF16),
        compiler_params=_cparams(("arbitrary", "arbitrary")),
        name="pool_mix",
    )(a, a, prefix, pw, sc)


def _attend(q, k, v, bias_ref, col_bias, o_ref):
    qb = q.shape[0]
    lane_hi = lax.broadcasted_iota(jnp.int32, (qb, 2 * HEAD_DIM), 1) >= HEAD_DIM
    for hp in range(ATT_HEADS // 2):
        sl = slice(hp * 2 * HEAD_DIM, (hp + 1) * 2 * HEAD_DIM)
        q2, k2, v2 = q[:, sl], k[:, sl], v[:, sl]
        out = None
        for sub in range(2):
            qm = jnp.where(lane_hi == (sub == 1), q2, jnp.zeros_like(q2))
            s = _nt_dot(qm, k2) + bias_ref[hp * 2 + sub]
            if col_bias is not None:
                s = s + col_bias
            m = jnp.max(s, axis=-1, keepdims=True)
            p = jnp.exp(s - m)
            l = jnp.sum(p, axis=-1, keepdims=True)
            o = jnp.dot(p.astype(BF16), v2, preferred_element_type=F32) / l
            out = o if sub == 0 else jnp.where(lane_hi, o, out)
        o_ref[:, sl] = out.astype(BF16)


def _attn_prompt_kernel(q_ref, k0_ref, k1_ref, k2_ref, v0_ref, v1_ref, v2_ref, bias_ref, o_ref):
    t = pl.program_id(1)
    qb = q_ref.shape[0]
    k = jnp.concatenate([k0_ref[...], k1_ref[...], k2_ref[...]], axis=0).astype(BF16)
    v = jnp.concatenate([v0_ref[...], v1_ref[...], v2_ref[...]], axis=0).astype(BF16)
    key_row = (t - 2) * qb + lax.broadcasted_iota(jnp.int32, (1, 3 * qb), 1)
    col_bias = jnp.where(key_row >= 0, 0.0, NEG_BIG).astype(F32)
    _attend(q_ref[...], k, v, bias_ref, col_bias, o_ref)


def _attn_prompt(q, k, v, bias, n_streams, t_len):
    qb = ATT_QB
    nt = t_len // qb
    cur = lambda b, t: (b * nt + t, 0)
    prev1 = lambda b, t: (b * nt + jnp.maximum(t - 1, 0), 0)
    prev2 = lambda b, t: (b * nt + jnp.maximum(t - 2, 0), 0)
    blk = lambda im: pl.BlockSpec((qb, ATT_W), im)
    return pl.pallas_call(
        _attn_prompt_kernel,
        grid=(n_streams, nt),
        in_specs=[blk(cur), blk(prev2), blk(prev1), blk(cur), blk(prev2), blk(prev1), blk(cur),
                  _const_spec(bias.shape)],
        out_specs=blk(cur),
        out_shape=jax.ShapeDtypeStruct((n_streams * t_len, ATT_W), BF16),
        compiler_params=_cparams(("arbitrary", "arbitrary")),
        name="attn_prompt",
    )(q, k, k, k, v, v, v, bias)


def _attn_sample_kernel(q_ref, kn_ref, vn_ref, kc_ref, vc_ref, bias_ref, o_ref):
    k = jnp.concatenate([kc_ref[0], kn_ref[...]], axis=0).astype(BF16)
    v = jnp.concatenate([vc_ref[0], vn_ref[...]], axis=0).astype(BF16)
    _attend(q_ref[...], k, v, bias_ref, None, o_ref)


def _attn_sample(q, k, v, kc, vc, bias, row0, n_streams, s_len):
    b0 = row0 // s_len
    n_cache = kc.shape[1]
    new = pl.BlockSpec((s_len, ATT_W), lambda s: (b0 + s, 0))
    cache = pl.BlockSpec((1, n_cache, ATT_W), lambda s: (s, 0, 0))
    return pl.pallas_call(
        _attn_sample_kernel,
        grid=(n_streams,),
        in_specs=[new, new, new, cache, cache, _const_spec(bias.shape)],
        out_specs=pl.BlockSpec((s_len, ATT_W), lambda s: (s, 0)),
        out_shape=jax.ShapeDtypeStruct((n_streams * s_len, ATT_W), BF16),
        compiler_params=_cparams(("arbitrary",)),
        name="attn_sample",
    )(q, k, v, kc, vc, bias)


def _merge_kernel(x_ref, ya_ref, yb_ref, yc_ref, gmix_ref, wg_ref, bg_ref, wa_ref, wb_ref, wc_ref,
                  wo_ref, gffn_ref, wq_ref, sk_ref, h_ref, hn_ref, st_ref):
    x = x_ref[...]
    xn = (x * lax.rsqrt(jnp.mean(x * x, axis=-1, keepdims=True) + EPS) * gmix_ref[...]).astype(BF16)
    merged = None
    for b, (y_ref, w_ref) in enumerate(((ya_ref, wa_ref), (yb_ref, wb_ref), (yc_ref, wc_ref))):
        cols = slice(b * D_MODEL, (b + 1) * D_MODEL)
        gate = jax.nn.sigmoid(jnp.dot(xn, wg_ref[:, cols], preferred_element_type=F32) + bg_ref[:, cols])
        term = gate * jnp.dot(y_ref[...], w_ref[...], preferred_element_type=F32)
        merged = term if merged is None else merged + term
    h = x + jnp.dot(merged.astype(BF16), wo_ref[...], preferred_element_type=F32)
    h_ref[...] = h
    hn = (h * lax.rsqrt(jnp.mean(h * h, axis=-1, keepdims=True) + EPS) * gffn_ref[...]).astype(BF16)
    hn_ref[...] = hn
    qp = jnp.dot(hn, wq_ref[...], preferred_element_type=F32).astype(BF16)
    for hp in range(2 * PEER_HEADS):
        st_ref[hp] = _nt_dot(sk_ref[hp % 2], qp[:, hp * PK_HALF:(hp + 1) * PK_HALF])


def _merge(x, ya, yb, yc, gmix, wg, bg, wa, wb, wc, wo, gffn, wq, sk):
    n = x.shape[0]
    tm = ROW_TILE
    row = lambda w: pl.BlockSpec((tm, w), lambda i: (i, 0))
    nq = wq.shape[1]
    return pl.pallas_call(
        _merge_kernel,
        grid=(n // tm,),
        in_specs=[row(D_MODEL), row(POOL_W), row(SGU_W), row(ATT_W),
                  _const_spec((1, D_MODEL)), _const_spec(wg.shape), _const_spec(bg.shape),
                  _const_spec(wa.shape), _const_spec(wb.shape), _const_spec(wc.shape),
                  _const_spec(wo.shape), _const_spec((1, D_MODEL)), _const_spec(wq.shape),
                  _const_spec(sk.shape)],
        out_specs=[row(D_MODEL), row(D_MODEL),
                   pl.BlockSpec((2 * PEER_HEADS, N_KEYS, tm), lambda i: (0, 0, i))],
        out_shape=[jax.ShapeDtypeStruct((n, D_MODEL), F32), jax.ShapeDtypeStruct((n, D_MODEL), BF16),
                   jax.ShapeDtypeStruct((2 * PEER_HEADS, N_KEYS, n), F32)],
        compiler_params=_cparams(("arbitrary",)),
        name="merge",
    )(x, ya, yb, yc, gmix, wg, bg, wa, wb, wc, wo, gffn, wq, sk)


def _top16(x, iota, payloads=()):
    n = x.shape[0]
    vals, idxs = [], []
    picked = [[] for _ in payloads]
    for _ in range(PK_TOPK):
        m = jnp.max(x, axis=0, keepdims=True)
        idx = jnp.min(jnp.where(x == m, iota, float(n)), axis=0, keepdims=True)
        hit = iota == idx
        vals.append(m)
        idxs.append(idx)
        for lst, p in zip(picked, payloads):
            lst.append(jnp.sum(jnp.where(hit, p, 0.0), axis=0, keepdims=True))
        x = jnp.where(hit, -jnp.inf, x)
    return vals, idxs, picked


_PAIRS = [(a, b) for a in range(PK_TOPK) for b in range(PK_TOPK) if (a + 1) * (b + 1) <= PK_TOPK]
_PAIR_ROWS = -(-len(_PAIRS) // SUBLANES) * SUBLANES


def _topk_kernel(st_ref, i_ref, j_ref, g_ref):
    tt = st_ref.shape[2]
    iota_k = lax.broadcasted_iota(jnp.int32, (N_KEYS, tt), 0).astype(F32)
    iota_c = lax.broadcasted_iota(jnp.int32, (_PAIR_ROWS, tt), 0).astype(F32)
    pad = _PAIR_ROWS - len(_PAIRS)
    neg_row = jnp.full((1, tt), -jnp.inf, F32)
    zero_row = jnp.zeros((1, tt), F32)
    i_rows, j_rows, g_rows = [], [], []
    for h in range(PEER_HEADS):
        v0, i0, _ = _top16(st_ref[2 * h], iota_k)
        v1, i1, _ = _top16(st_ref[2 * h + 1], iota_k)
        cand = jnp.concatenate([v0[a] + v1[b] for a, b in _PAIRS] + [neg_row] * pad, axis=0)
        cand_i = jnp.concatenate([i0[a] for a, _ in _PAIRS] + [zero_row] * pad, axis=0)
        cand_j = jnp.concatenate([i1[b] for _, b in _PAIRS] + [zero_row] * pad, axis=0)
        tv, _, (ti, tj) = _top16(cand, iota_c, (cand_i, cand_j))
        e = jnp.exp(jnp.concatenate(tv, axis=0) - tv[0])
        g_rows.append(e / jnp.sum(e, axis=0, keepdims=True))
        i_rows += ti
        j_rows += tj
    i_ref[...] = jnp.concatenate(i_rows, axis=0).T
    j_ref[...] = jnp.concatenate(j_rows, axis=0).T
    g_ref[...] = jnp.concatenate(g_rows, axis=0).T


def _topk(st):
    n = st.shape[2]
    tt = TOPK_TT
    nsel = PEER_HEADS * PK_TOPK
    out = pl.BlockSpec((tt, nsel), lambda i: (i, 0))
    return pl.pallas_call(
        _topk_kernel,
        grid=(n // tt,),
        in_specs=[pl.BlockSpec((2 * PEER_HEADS, N_KEYS, tt), lambda i: (0, 0, i))],
        out_specs=[out, out, out],
        out_shape=[jax.ShapeDtypeStruct((n, nsel), F32)] * 3,
        compiler_params=_cparams(("arbitrary",)),
        name="peer_topk",
    )(st)


def _gelu(x):
    return 0.5 * x * (1.0 + lax.erf(x * (2.0 ** -0.5)))


def _peer_kernel(hn_ref, h_ref, i_ref, j_ref, g_ref, u_ref, v_ref, o_ref, hs_ref, acc_ref):
    s = pl.program_id(1)
    tb = hn_ref.shape[0]
    rpc = PEER_ROWS_PER_CHUNK

    @pl.when(s < PEER_NC)
    def _scores():
        hc = _nt_dot(hn_ref[...], u_ref[...])
        slab = hs_ref.at[s]
        for ib in range(rpc):
            slab[pl.ds(ib, tb, stride=rpc), :] = hc[:, ib * N_KEYS:(ib + 1) * N_KEYS]

    @pl.when(s == PEER_NC - 1)
    def _weights():
        sub = lax.broadcasted_iota(jnp.int32, (N_KEYS, N_KEYS), 0).astype(F32)

        def one_token(n):
            irow = i_ref[pl.ds(n, 1), :]
            jrow = j_ref[pl.ds(n, 1), :]
            grow = g_ref[pl.ds(n, 1), :]
            at = jnp.where(sub == irow, 1.0, 0.0).astype(BF16)
            g_hi, g_lo = _split_bf16(jnp.where(sub == jrow, grow, 0.0))
            gmat = _nt_dot(jnp.concatenate([at, at], axis=1),
                           jnp.concatenate([g_hi, g_lo], axis=1))
            r0 = pl.multiple_of(n * rpc, rpc)
            for c in range(PEER_NC):
                hv = hs_ref[c, pl.ds(r0, rpc), :]
                hs_ref[c, pl.ds(r0, rpc), :] = _gelu(hv) * gmat[c * rpc:(c + 1) * rpc, :]

        def body(nb, carry):
            for t in range(PEER_TOKEN_UNROLL):
                one_token(nb * PEER_TOKEN_UNROLL + t)
            return carry

        lax.fori_loop(0, tb // PEER_TOKEN_UNROLL, body, 0)

    @pl.when(s >= PEER_NC)
    def _combine():
        c = s - PEER_NC
        slab = hs_ref.at[c]
        wc = jnp.concatenate([slab[pl.ds(ib, tb, stride=rpc), :] for ib in range(rpc)], axis=1)
        part = jnp.dot(wc.astype(BF16), v_ref[...], preferred_element_type=F32)

        @pl.when(c == 0)
        def _():
            acc_ref[...] = part

        @pl.when(c > 0)
        def _():
            acc_ref[...] += part

    @pl.when(s == 2 * PEER_NC - 1)
    def _finish():
        o_ref[...] = h_ref[...] + acc_ref[...]


def _peer(hn, h, isel, jsel, gsel, u, v, tb):
    n = hn.shape[0]
    nsel = PEER_HEADS * PK_TOPK
    nc = PEER_NC
    row = lambda w: pl.BlockSpec((tb, w), lambda b, s: (b, 0))
    return pl.pallas_call(
        _peer_kernel,
        grid=(n // tb, 2 * nc),
        in_specs=[row(D_MODEL), row(D_MODEL), row(nsel), row(nsel), row(nsel),
                  pl.BlockSpec((PEER_EC, D_MODEL), lambda b, s: (jnp.minimum(s, nc - 1), 0)),
                  pl.BlockSpec((PEER_EC, D_MODEL), lambda b, s: (jnp.maximum(s - nc, 0), 0))],
        out_specs=row(D_MODEL),
        out_shape=jax.ShapeDtypeStruct((n, D_MODEL), F32),
        scratch_shapes=[pltpu.VMEM((nc, tb * PEER_ROWS_PER_CHUNK, N_KEYS), F32),
                        pltpu.VMEM((tb, D_MODEL), F32)],
        compiler_params=_cparams(("arbitrary", "arbitrary")),
        name="peer_experts",
    )(hn, h, isel, jsel, gsel, u, v)


def _bias_kernel(g_ref, full_ref, band_ref):
    _, nq, nk = full_ref.shape
    w = g_ref.shape[1]
    qc = lax.broadcasted_iota(jnp.int32, (nq, nk), 0) // CHUNK
    kc = lax.broadcasted_iota(jnp.int32, (nq, nk), 1) // CHUNK
    band = jnp.abs(2 * (kc - qc) - BAND // CHUNK) <= BAND // CHUNK
    for h in range(ATT_HEADS):
        line = jnp.broadcast_to(g_ref[h:h + 1, :], (nq, w))
        t = pltpu.roll(line, w - nq + 1, 1, stride=1, stride_axis=0)[:, :nk]
        full_ref[h] = t
        band_ref[h] = jnp.where(band, t, NEG_BIG)


def _rel_bias(table, nq, nk):
    w = 1024
    assert nq + nk - 1 <= w
    dist = nk - 1 - jnp.arange(w)
    line = table[:, jnp.clip(dist, -REL_CLIP, REL_CLIP) + REL_CLIP].astype(F32)
    shape = jax.ShapeDtypeStruct((ATT_HEADS, nq, nk), F32)
    return pl.pallas_call(
        _bias_kernel,
        out_shape=[shape, shape],
        compiler_params=pltpu.CompilerParams(vmem_limit_bytes=VMEM_LIMIT),
        name="rel_bias",
    )(line)


def _block_diag(blocks):
    g, r, c = blocks.shape
    eye = jnp.eye(g, dtype=blocks.dtype)
    return (eye[:, None, :, None] * blocks[:, :, None, :]).reshape(g * r, g * c)


def _sgu_mats(w_s, b_s, length):
    g = w_s.shape[0]
    reps = SGU_CHUNK // length
    tri = jnp.tril(jnp.ones((length, length), bool))
    ws = jnp.where(tri[None], w_s[:, :length, :length], 0.0)
    eye = jnp.eye(reps, dtype=ws.dtype)
    mats = (eye[None, :, None, :, None] * ws[:, None, :, None, :]).reshape(g, SGU_CHUNK, SGU_CHUNK)
    bias = jnp.tile(b_s[:, :length], (1, reps))
    bias = jnp.repeat(bias.T, SGU_GW, axis=1)
    return mats, bias


def kernel(x_prompt, x_sample, state_pool, cache_k, cache_v, g_mix, w_in, pool_w, pool_scale, sgu_norm, sgu_w, sgu_b, q_norm, k_norm, rel_table, w_gate, b_gate, w_br_a, w_br_b, w_br_c, w_out, g_ffn, peer_wq, peer_subkeys, peer_u, peer_v):
    depth = g_mix.shape[0]
    bsz, seq, d = x_prompt.shape
    dbsz, dseq, _ = x_sample.shape
    n_p, n_s = bsz * seq, dbsz * dseq
    n_cache = cache_k.shape[2]
    assert seq % ATT_QB == 0 and n_p % ROW_TILE == 0 and n_s % ROW_TILE == 0
    assert dseq <= SGU_CHUNK and SGU_CHUNK % dseq == 0 and dseq >= POOL_STATE + 1 and dseq % SUBLANES == 0
    assert n_cache == 2 * ATT_QB and dseq <= ATT_QB
    x = jnp.concatenate([x_prompt.reshape(n_p, d), x_sample.reshape(n_s, d)], axis=0)

    hsum = _block_diag(jnp.full((ATT_HEADS, HEAD_DIM, HEAD_DIM), 1.0 / HEAD_DIM, F32)).astype(BF16)
    zero_prefix = jnp.zeros((bsz, HALO, POOL_W), F32)
    keep = min(BAND, seq)
    outs = {name: [] for name in ("pool_p", "pool_s", "k_p", "v_p", "k_s", "v_s", "sgu_s")}

    for l in range(depth):
        row = lambda a: a[l].reshape(1, -1)
        m_p, b_p = _sgu_mats(sgu_w[l], sgu_b[l], SGU_CHUNK)
        m_s, b_s = _sgu_mats(sgu_w[l], sgu_b[l], dseq)
        ms = jnp.stack([m_p, m_s]).astype(BF16)
        bs = jnp.stack([b_p, b_s])
        a, yb, vn, q, k, v = _in_proj(
            x, row(g_mix), w_in[l].astype(BF16), row(sgu_norm),
            jnp.tile(q_norm[l], ATT_HEADS).reshape(1, -1), jnp.tile(k_norm[l], ATT_HEADS).reshape(1, -1),
            hsum, ms, bs, n_p // ROW_TILE)

        pw = _block_diag(pool_w[l]).astype(BF16)
        sample_prefix = jnp.pad(state_pool[l], ((0, 0), (HALO - POOL_STATE, 0), (0, 0)))
        ya_p = _pool(a, zero_prefix, pw, row(pool_scale), 0, bsz, seq, ROW_TILE)
        ya_s = _pool(a, sample_prefix, pw, row(pool_scale), n_p, dbsz, dseq, dseq)

        bias_full, bias_band = _rel_bias(rel_table[l], ATT_QB, 3 * ATT_QB)
        yc_p = _attn_prompt(q, k, v, bias_band, bsz, seq)
        yc_s = _attn_sample(q, k, v, cache_k[l].reshape(dbsz, n_cache, ATT_W),
                            cache_v[l].reshape(dbsz, n_cache, ATT_W),
                            bias_full[:, :dseq, :n_cache + dseq], n_p, dbsz, dseq)

        h, hn, st = _merge(
            x, jnp.concatenate([ya_p, ya_s], axis=0), yb, jnp.concatenate([yc_p, yc_s], axis=0),
            row(g_mix), w_gate[l].astype(BF16), row(b_gate), w_br_a[l].astype(BF16),
            w_br_b[l].astype(BF16), w_br_c[l].astype(BF16), w_out[l].astype(BF16), row(g_ffn),
            peer_wq[l].astype(BF16), peer_subkeys[l].astype(BF16))
        isel, jsel, gsel = _topk(st)
        x = _peer(hn, h, isel, jsel, gsel, peer_u[l].astype(BF16), peer_v[l].astype(BF16), ROW_TILE)

        a_p = a[:n_p].reshape(bsz, seq, POOL_W)
        a_s = a[n_p:].reshape(dbsz, dseq, POOL_W)
        outs["pool_p"].append(a_p[:, seq - POOL_STATE:])
        outs["pool_s"].append(a_s[:, dseq - POOL_STATE:])
        k_p = k[:n_p].reshape(bsz, seq, ATT_HEADS, HEAD_DIM)
        v_p = v[:n_p].reshape(bsz, seq, ATT_HEADS, HEAD_DIM)
        outs["k_p"].append(k_p[:, seq - keep:])
        outs["v_p"].append(v_p[:, seq - keep:])
        outs["k_s"].append(k[n_p:].reshape(dbsz, dseq, ATT_HEADS, HEAD_DIM))
        outs["v_s"].append(v[n_p:].reshape(dbsz, dseq, ATT_HEADS, HEAD_DIM))
        outs["sgu_s"].append(vn[n_p:].reshape(dbsz, dseq, SGU_W))

    st = lambda name: jnp.stack(outs[name])
    return (x[:n_p].reshape(bsz, seq, d), x[n_p:].reshape(dbsz, dseq, d), st("pool_p"), st("pool_s"),
            st("k_p"), st("v_p"), st("k_s"), st("v_s"), st("sgu_s"))
```

```python
import functools

import jax
import jax.numpy as jnp
from jax import lax
from jax.experimental import pallas as pl
from jax.experimental.pallas import tpu as pltpu

F32 = jnp.float32
BF16 = jnp.bfloat16

EPS = 1e-6
D_MODEL = 1024
POOL_W = 256
POOL_GW = 64
POOL_STATE = 15
SGU_W = 256
SGU_GW = 64
SGU_CHUNK = 128
ATT_HEADS = 8
HEAD_DIM = 64
ATT_W = ATT_HEADS * HEAD_DIM
CHUNK = 64
BAND = 512
REL_CLIP = 128
N_BRANCH = 3
PEER_HEADS = 8
N_KEYS = 128
PK_HALF = 128
PK_TOPK = 16
NEG_BIG = -1e30

LANES = 128
SUBLANES = 8
ROW_TILE = 256
ATT_QB = 256
HALO = 16
TOPK_TT = 128
PEER_EC = 1024
PEER_ROWS_PER_CHUNK = PEER_EC // N_KEYS
PEER_NC = N_KEYS * N_KEYS // PEER_EC
PEER_TOKEN_UNROLL = 8
PEER_TB = 512
VMEM_LIMIT = 56 * 1024 * 1024
PEER_VMEM_LIMIT = 60 * 1024 * 1024


def _cparams(sem):
    return pltpu.CompilerParams(dimension_semantics=sem, vmem_limit_bytes=VMEM_LIMIT)


def _const_spec(shape):
    n = len(shape)
    return pl.BlockSpec(shape, lambda *_: (0,) * n)


def _nt_dot(a, b):
    return lax.dot_general(a, b, (((1,), (1,)), ((), ())), preferred_element_type=F32)


def _split_bf16(x):
    hi = x.astype(BF16)
    lo = (x - hi.astype(F32)).astype(BF16)
    return hi, lo


def _in_proj_kernel(x_ref, gmix_ref, win_ref, sgn_ref, qn_ref, kn_ref, hsum_ref, ms_ref, bs_ref,
                    a_ref, yb_ref, vn_ref, q_ref, k_ref, v_ref):
    x = x_ref[...]
    xn = x * lax.rsqrt(jnp.mean(x * x, axis=-1, keepdims=True) + EPS) * gmix_ref[...]
    z = jnp.dot(xn.astype(BF16), win_ref[...], preferred_element_type=F32)
    a_ref[...] = z[:, 0:POOL_W]
    u = z[:, POOL_W:POOL_W + SGU_W]
    vb = z[:, POOL_W + SGU_W:POOL_W + 2 * SGU_W]
    vn = vb * lax.rsqrt(jnp.mean(vb * vb, axis=-1, keepdims=True) + EPS) * sgn_ref[...]
    vn_ref[...] = vn
    vnb = vn.astype(BF16)
    lane_group = lax.broadcasted_iota(jnp.int32, (SGU_CHUNK, SGU_W), 1) // SGU_GW
    rows = x.shape[0]
    for c in range(rows // SGU_CHUNK):
        sl = slice(c * SGU_CHUNK, (c + 1) * SGU_CHUNK)
        vc = vnb[sl, :]
        mixed = jnp.dot(ms_ref[0, 0], vc, preferred_element_type=F32)
        for g in range(1, SGU_W // SGU_GW):
            mg = jnp.dot(ms_ref[0, g], vc, preferred_element_type=F32)
            mixed = jnp.where(lane_group == g, mg, mixed)
        yb_ref[sl, :] = (u[sl, :] * (mixed + bs_ref[0])).astype(BF16)

    base = POOL_W + 2 * SGU_W
    q = z[:, base:base + ATT_W]
    k = z[:, base + ATT_W:base + 2 * ATT_W]
    v_ref[...] = z[:, base + 2 * ATT_W:base + 3 * ATT_W]

    def head_norm(t, w_ref):
        hi, lo = _split_bf16(t * t)
        m = (jnp.dot(hi, hsum_ref[...], preferred_element_type=F32)
             + jnp.dot(lo, hsum_ref[...], preferred_element_type=F32))
        return t * lax.rsqrt(m + EPS) * w_ref[...]

    q_ref[...] = (head_norm(q, qn_ref) * (HEAD_DIM ** -0.5)).astype(BF16)
    k_ref[...] = head_norm(k, kn_ref)


def _in_proj(x, gmix, win, sgn, qn, kn, hsum, ms, bs, n_prompt_tiles):
    n = x.shape[0]
    tm = ROW_TILE
    in_w = win.shape[1]
    row = lambda w: pl.BlockSpec((tm, w), lambda i: (i, 0))
    sel = lambda i: (jnp.where(i >= n_prompt_tiles, 1, 0), 0, 0, 0)
    sel3 = lambda i: (jnp.where(i >= n_prompt_tiles, 1, 0), 0, 0)
    return pl.pallas_call(
        _in_proj_kernel,
        grid=(n // tm,),
        in_specs=[row(D_MODEL), _const_spec((1, D_MODEL)), _const_spec((D_MODEL, in_w)),
                  _const_spec((1, SGU_W)), _const_spec((1, ATT_W)), _const_spec((1, ATT_W)),
                  _const_spec((ATT_W, ATT_W)),
                  pl.BlockSpec((1, SGU_W // SGU_GW, SGU_CHUNK, SGU_CHUNK), sel),
                  pl.BlockSpec((1, SGU_CHUNK, SGU_W), sel3)],
        out_specs=[row(POOL_W), row(SGU_W), row(SGU_W), row(ATT_W), row(ATT_W), row(ATT_W)],
        out_shape=[jax.ShapeDtypeStruct((n, POOL_W), F32), jax.ShapeDtypeStruct((n, SGU_W), BF16),
                   jax.ShapeDtypeStruct((n, SGU_W), F32), jax.ShapeDtypeStruct((n, ATT_W), BF16),
                   jax.ShapeDtypeStruct((n, ATT_W), F32), jax.ShapeDtypeStruct((n, ATT_W), F32)],
        compiler_params=_cparams(("arbitrary",)),
        name="in_proj",
    )(x, gmix, win, sgn, qn, kn, hsum, ms, bs)


def _pool_kernel(a_ref, prev_ref, pre_ref, pw_ref, sc_ref, y_ref):
    t = pl.program_id(1)
    a = a_ref[...]
    tm = a.shape[0]
    halo = jnp.where(t == 0, pre_ref[0], prev_ref[...])
    e = jnp.concatenate([halo, a], axis=0)
    s2 = e[1:] + e[:-1]
    s4 = s2[2:] + s2[:-2]
    s8 = s4[4:] + s4[:-4]
    s16 = s8[8:] + s8[:-8]
    lg = lax.broadcasted_iota(jnp.int32, (tm, POOL_W), 1) // POOL_GW
    mean = jnp.where(lg == 0, s2[15:] * 0.5,
                     jnp.where(lg == 1, s4[13:] * 0.25,
                               jnp.where(lg == 2, s8[9:] * 0.125, s16[1:] * 0.0625)))
    pooled = mean - a
    y = jnp.dot(pooled.astype(BF16), pw_ref[...], preferred_element_type=F32) * sc_ref[...]
    y_ref[...] = y.astype(BF16)


def _pool(a, prefix, pw, sc, row0, n_streams, t_len, tm):
    nt = t_len // tm
    b0 = row0 // tm
    hb = tm // HALO
    return pl.pallas_call(
        _pool_kernel,
        grid=(n_streams, nt),
        in_specs=[pl.BlockSpec((tm, POOL_W), lambda b, t: (b0 + b * nt + t, 0)),
                  pl.BlockSpec((HALO, POOL_W), lambda b, t: (jnp.maximum((b0 + b * nt + t) * hb - 1, 0), 0)),
                  pl.BlockSpec((1, HALO, POOL_W), lambda b, t: (b, 0, 0)),
                  _const_spec((POOL_W, POOL_W)), _const_spec((1, POOL_W))],
        out_specs=pl.BlockSpec((tm, POOL_W), lambda b, t: (b * nt + t, 0)),
        out_shape=jax.ShapeDtypeStruct((n_streams * t_len, POOL_W), BF16),
        compiler_params=_cparams(("arbitrary", "arbitrary")),
        name="pool_mix",
    )(a, a, prefix, pw, sc)


def _attend(q, k, v, bias_ref, col_bias, o_ref):
    qb = q.shape[0]
    lane_hi = lax.broadcasted_iota(jnp.int32, (qb, 2 * HEAD_DIM), 1) >= HEAD_DIM
    for hp in range(ATT_HEADS // 2):
        sl = slice(hp * 2 * HEAD_DIM, (hp + 1) * 2 * HEAD_DIM)
        q2, k2, v2 = q[:, sl], k[:, sl], v[:, sl]
        out = None
        for sub in range(2):
            qm = jnp.where(lane_hi == (sub == 1), q2, jnp.zeros_like(q2))
            s = _nt_dot(qm, k2) + bias_ref[hp * 2 + sub]
            if col_bias is not None:
                s = s + col_bias
            m = jnp.max(s, axis=-1, keepdims=True)
            p = jnp.exp(s - m)
            l = jnp.sum(p, axis=-1, keepdims=True)
            o = jnp.dot(p.astype(BF16), v2, preferred_element_type=F32) / l
            out = o if sub == 0 else jnp.where(lane_hi, o, out)
        o_ref[:, sl] = out.astype(BF16)


def _attn_prompt_kernel(q_ref, k0_ref, k1_ref, k2_ref, v0_ref, v1_ref, v2_ref, bias_ref, o_ref):
    t = pl.program_id(1)
    qb = q_ref.shape[0]
    k = jnp.concatenate([k0_ref[...], k1_ref[...], k2_ref[...]], axis=0).astype(BF16)
    v = jnp.concatenate([v0_ref[...], v1_ref[...], v2_ref[...]], axis=0).astype(BF16)
    key_row = (t - 2) * qb + lax.broadcasted_iota(jnp.int32, (1, 3 * qb), 1)
    col_bias = jnp.where(key_row >= 0, 0.0, NEG_BIG).astype(F32)
    _attend(q_ref[...], k, v, bias_ref, col_bias, o_ref)


def _attn_prompt(q, k, v, bias, n_streams, t_len):
    qb = ATT_QB
    nt = t_len // qb
    cur = lambda b, t: (b * nt + t, 0)
    prev1 = lambda b, t: (b * nt + jnp.maximum(t - 1, 0), 0)
    prev2 = lambda b, t: (b * nt + jnp.maximum(t - 2, 0), 0)
    blk = lambda im: pl.BlockSpec((qb, ATT_W), im)
    return pl.pallas_call(
        _attn_prompt_kernel,
        grid=(n_streams, nt),
        in_specs=[blk(cur), blk(prev2), blk(prev1), blk(cur), blk(prev2), blk(prev1), blk(cur),
                  _const_spec(bias.shape)],
        out_specs=blk(cur),
        out_shape=jax.ShapeDtypeStruct((n_streams * t_len, ATT_W), BF16),
        compiler_params=_cparams(("arbitrary", "arbitrary")),
        name="attn_prompt",
    )(q, k, k, k, v, v, v, bias)


def _attn_sample_kernel(q_ref, kn_ref, vn_ref, kc_ref, vc_ref, bias_ref, o_ref):
    k = jnp.concatenate([kc_ref[0], kn_ref[...]], axis=0).astype(BF16)
    v = jnp.concatenate([vc_ref[0], vn_ref[...]], axis=0).astype(BF16)
    _attend(q_ref[...], k, v, bias_ref, None, o_ref)


def _attn_sample(q, k, v, kc, vc, bias, row0, n_streams, s_len):
    b0 = row0 // s_len
    n_cache = kc.shape[1]
    new = pl.BlockSpec((s_len, ATT_W), lambda s: (b0 + s, 0))
    cache = pl.BlockSpec((1, n_cache, ATT_W), lambda s: (s, 0, 0))
    return pl.pallas_call(
        _attn_sample_kernel,
        grid=(n_streams,),
        in_specs=[new, new, new, cache, cache, _const_spec(bias.shape)],
        out_specs=pl.BlockSpec((s_len, ATT_W), lambda s: (s, 0)),
        out_shape=jax.ShapeDtypeStruct((n_streams * s_len, ATT_W), BF16),
        compiler_params=_cparams(("arbitrary",)),
        name="attn_sample",
    )(q, k, v, kc, vc, bias)


def _merge_kernel(x_ref, ya_ref, yb_ref, yc_ref, gmix_ref, wg_ref, bg_ref, wa_ref, wb_ref, wc_ref,
                  wo_ref, gffn_ref, wq_ref, sk_ref, h_ref, hn_ref, st_ref):
    x = x_ref[...]
    xn = (x * lax.rsqrt(jnp.mean(x * x, axis=-1, keepdims=True) + EPS) * gmix_ref[...]).astype(BF16)
    merged = None
    for b, (y_ref, w_ref) in enumerate(((ya_ref, wa_ref), (yb_ref, wb_ref), (yc_ref, wc_ref))):
        cols = slice(b * D_MODEL, (b + 1) * D_MODEL)
        gate = jax.nn.sigmoid(jnp.dot(xn, wg_ref[:, cols], preferred_element_type=F32) + bg_ref[:, cols])
        term = gate * jnp.dot(y_ref[...], w_ref[...], preferred_element_type=F32)
        merged = term if merged is None else merged + term
    h = x + jnp.dot(merged.astype(BF16), wo_ref[...], preferred_element_type=F32)
    h_ref[...] = h
    hn = (h * lax.rsqrt(jnp.mean(h * h, axis=-1, keepdims=True) + EPS) * gffn_ref[...]).astype(BF16)
    hn_ref[...] = hn
    qp = jnp.dot(hn, wq_ref[...], preferred_element_type=F32).astype(BF16)
    for hp in range(2 * PEER_HEADS):
        st_ref[hp] = _nt_dot(sk_ref[hp % 2], qp[:, hp * PK_HALF:(hp + 1) * PK_HALF])


def _merge(x, ya, yb, yc, gmix, wg, bg, wa, wb, wc, wo, gffn, wq, sk):
    n = x.shape[0]
    tm = ROW_TILE
    row = lambda w: pl.BlockSpec((tm, w), lambda i: (i, 0))
    nq = wq.shape[1]
    return pl.pallas_call(
        _merge_kernel,
        grid=(n // tm,),
        in_specs=[row(D_MODEL), row(POOL_W), row(SGU_W), row(ATT_W),
                  _const_spec((1, D_MODEL)), _const_spec(wg.shape), _const_spec(bg.shape),
                  _const_spec(wa.shape), _const_spec(wb.shape), _const_spec(wc.shape),
                  _const_spec(wo.shape), _const_spec((1, D_MODEL)), _const_spec(wq.shape),
                  _const_spec(sk.shape)],
        out_specs=[row(D_MODEL), row(D_MODEL),
                   pl.BlockSpec((2 * PEER_HEADS, N_KEYS, tm), lambda i: (0, 0, i))],
        out_shape=[jax.ShapeDtypeStruct((n, D_MODEL), F32), jax.ShapeDtypeStruct((n, D_MODEL), BF16),
                   jax.ShapeDtypeStruct((2 * PEER_HEADS, N_KEYS, n), F32)],
        compiler_params=_cparams(("arbitrary",)),
        name="merge",
    )(x, ya, yb, yc, gmix, wg, bg, wa, wb, wc, wo, gffn, wq, sk)


def _top16(x, iota, payloads=()):
    n = x.shape[0]
    vals, idxs = [], []
    picked = [[] for _ in payloads]
    for _ in range(PK_TOPK):
        m = jnp.max(x, axis=0, keepdims=True)
        idx = jnp.min(jnp.where(x == m, iota, float(n)), axis=0, keepdims=True)
        hit = iota == idx
        vals.append(m)
        idxs.append(idx)
        for lst, p in zip(picked, payloads):
            lst.append(jnp.sum(jnp.where(hit, p, 0.0), axis=0, keepdims=True))
        x = jnp.where(hit, -jnp.inf, x)
    return vals, idxs, picked


_PAIRS = [(a, b) for a in range(PK_TOPK) for b in range(PK_TOPK) if (a + 1) * (b + 1) <= PK_TOPK]
_PAIR_ROWS = -(-len(_PAIRS) // SUBLANES) * SUBLANES


def _topk_kernel(st_ref, i_ref, j_ref, g_ref):
    tt = st_ref.shape[2]
    iota_k = lax.broadcasted_iota(jnp.int32, (N_KEYS, tt), 0).astype(F32)
    iota_c = lax.broadcasted_iota(jnp.int32, (_PAIR_ROWS, tt), 0).astype(F32)
    pad = _PAIR_ROWS - len(_PAIRS)
    neg_row = jnp.full((1, tt), -jnp.inf, F32)
    zero_row = jnp.zeros((1, tt), F32)
    i_rows, j_rows, g_rows = [], [], []
    for h in range(PEER_HEADS):
        v0, i0, _ = _top16(st_ref[2 * h], iota_k)
        v1, i1, _ = _top16(st_ref[2 * h + 1], iota_k)
        cand = jnp.concatenate([v0[a] + v1[b] for a, b in _PAIRS] + [neg_row] * pad, axis=0)
        cand_i = jnp.concatenate([i0[a] for a, _ in _PAIRS] + [zero_row] * pad, axis=0)
        cand_j = jnp.concatenate([i1[b] for _, b in _PAIRS] + [zero_row] * pad, axis=0)
        tv, _, (ti, tj) = _top16(cand, iota_c, (cand_i, cand_j))
        e = jnp.exp(jnp.concatenate(tv, axis=0) - tv[0])
        g_rows.append(e / jnp.sum(e, axis=0, keepdims=True))
        i_rows += ti
        j_rows += tj
    i_ref[...] = jnp.concatenate(i_rows, axis=0).T
    j_ref[...] = jnp.concatenate(j_rows, axis=0).T
    g_ref[...] = jnp.concatenate(g_rows, axis=0).T


def _topk(st):
    n = st.shape[2]
    tt = TOPK_TT
    nsel = PEER_HEADS * PK_TOPK
    out = pl.BlockSpec((tt, nsel), lambda i: (i, 0))
    return pl.pallas_call(
        _topk_kernel,
        grid=(n // tt,),
        in_specs=[pl.BlockSpec((2 * PEER_HEADS, N_KEYS, tt), lambda i: (0, 0, i))],
        out_specs=[out, out, out],
        out_shape=[jax.ShapeDtypeStruct((n, nsel), F32)] * 3,
        compiler_params=_cparams(("arbitrary",)),
        name="peer_topk",
    )(st)


def _gelu(x):
    return 0.5 * x * (1.0 + lax.erf(x * (2.0 ** -0.5)))


def _peer_kernel(hn_ref, h_ref, i_ref, j_ref, g_ref, uv_ref, o_ref, hs_ref, acc_ref, gm_ref):
    s = pl.program_id(1)
    tb = hn_ref.shape[0]
    rpc = PEER_ROWS_PER_CHUNK

    @pl.when(s < PEER_NC)
    def _scores():
        hc = _nt_dot(hn_ref[...], uv_ref[...])
        slab = hs_ref.at[s]
        for ib in range(rpc):
            slab[pl.ds(ib, tb, stride=rpc), :] = hc[:, ib * N_KEYS:(ib + 1) * N_KEYS]

    @pl.when(s == PEER_NC - 1)
    def _weights():
        sub = lax.broadcasted_iota(jnp.int32, (N_KEYS, N_KEYS), 0).astype(F32).astype(BF16)
        one = jnp.ones((N_KEYS, N_KEYS), BF16)
        zero = jnp.zeros((N_KEYS, N_KEYS), BF16)

        def gate_matrix(n):
            irow = i_ref[pl.ds(n, 1), :].astype(BF16)
            jrow = j_ref[pl.ds(n, 1), :].astype(BF16)
            g_hi, g_lo = _split_bf16(0.5 * g_ref[pl.ds(n, 1), :])
            at = jnp.where(sub == irow, one, zero)
            jhit = sub == jrow
            b_hi = jnp.where(jhit, jnp.broadcast_to(g_hi, jhit.shape), zero)
            b_lo = jnp.where(jhit, jnp.broadcast_to(g_lo, jhit.shape), zero)
            return _nt_dot(jnp.concatenate([at, at], axis=1),
                           jnp.concatenate([b_hi, b_lo], axis=1))

        unroll = PEER_TOKEN_UNROLL
        n_groups = tb // unroll

        def make_gates(group, slot):
            group = jnp.minimum(group, n_groups - 1)
            for t in range(unroll):
                gm_ref[slot, t * N_KEYS:(t + 1) * N_KEYS, :] = gate_matrix(group * unroll + t)

        def apply_gates(group, slot):
            base = pl.multiple_of(group * (unroll * rpc), unroll * rpc)
            for c in range(PEER_NC):
                hv = hs_ref[c, pl.ds(base, unroll * rpc), :]
                gm = jnp.concatenate([gm_ref[slot, t * N_KEYS + c * rpc:t * N_KEYS + (c + 1) * rpc, :]
                                      for t in range(unroll)], axis=0)
                hs_ref[c, pl.ds(base, unroll * rpc), :] = (hv * gm) * (1.0 + lax.erf(hv * (2.0 ** -0.5)))

        def body(m, carry):
            apply_gates(2 * m, 0)
            make_gates(2 * m + 1, 1)
            apply_gates(2 * m + 1, 1)
            make_gates(2 * m + 2, 0)
            return carry

        acc_ref[...] = h_ref[...]
        make_gates(0, 0)
        lax.fori_loop(0, n_groups // 2, body, 0)

    @pl.when(s >= PEER_NC)
    def _combine():
        c = s - PEER_NC
        slab = hs_ref.at[c]
        wc = jnp.concatenate([slab[pl.ds(ib, tb, stride=rpc), :] for ib in range(rpc)], axis=1)
        acc_ref[...] += jnp.dot(wc.astype(BF16), uv_ref[...], preferred_element_type=F32)

    @pl.when(s == 2 * PEER_NC - 1)
    def _finish():
        o_ref[...] = acc_ref[...]


def _peer(hn, h, isel, jsel, gsel, uv):
    n = hn.shape[0]
    tb = PEER_TB
    nsel = PEER_HEADS * PK_TOPK
    nc = PEER_NC
    row = lambda w: pl.BlockSpec((tb, w), lambda b, s: (b, 0))
    return pl.pallas_call(
        _peer_kernel,
        grid=(n // tb, 2 * nc),
        in_specs=[row(D_MODEL), row(D_MODEL), row(nsel), row(nsel), row(nsel),
                  pl.BlockSpec((None, PEER_EC, D_MODEL), lambda b, s: (s // nc, s % nc, 0))],
        out_specs=row(D_MODEL),
        out_shape=jax.ShapeDtypeStruct((n, D_MODEL), F32),
        scratch_shapes=[pltpu.VMEM((nc, tb * PEER_ROWS_PER_CHUNK, N_KEYS), F32),
                        pltpu.VMEM((tb, D_MODEL), F32),
                        pltpu.VMEM((2, PEER_TOKEN_UNROLL * N_KEYS, N_KEYS), F32)],
        compiler_params=pltpu.CompilerParams(dimension_semantics=("arbitrary", "arbitrary"),
                                             vmem_limit_bytes=PEER_VMEM_LIMIT),
        name="peer_experts",
    )(hn, h, isel, jsel, gsel, uv)


def _bias_kernel(g_ref, full_ref, band_ref):
    _, nq, nk = full_ref.shape
    w = g_ref.shape[1]
    qc = lax.broadcasted_iota(jnp.int32, (nq, nk), 0) // CHUNK
    kc = lax.broadcasted_iota(jnp.int32, (nq, nk), 1) // CHUNK
    band = jnp.abs(2 * (kc - qc) - BAND // CHUNK) <= BAND // CHUNK
    for h in range(ATT_HEADS):
        line = jnp.broadcast_to(g_ref[h:h + 1, :], (nq, w))
        t = pltpu.roll(line, w - nq + 1, 1, stride=1, stride_axis=0)[:, :nk]
        full_ref[h] = t
        band_ref[h] = jnp.where(band, t, NEG_BIG)


def _rel_bias(table, nq, nk):
    w = 1024
    assert nq + nk - 1 <= w
    dist = nk - 1 - jnp.arange(w)
    line = table[:, jnp.clip(dist, -REL_CLIP, REL_CLIP) + REL_CLIP].astype(F32)
    shape = jax.ShapeDtypeStruct((ATT_HEADS, nq, nk), F32)
    return pl.pallas_call(
        _bias_kernel,
        out_shape=[shape, shape],
        compiler_params=pltpu.CompilerParams(vmem_limit_bytes=VMEM_LIMIT),
        name="rel_bias",
    )(line)


def _block_diag(blocks):
    g, r, c = blocks.shape
    eye = jnp.eye(g, dtype=blocks.dtype)
    return (eye[:, None, :, None] * blocks[:, :, None, :]).reshape(g * r, g * c)


def _sgu_mats(w_s, b_s, length):
    g = w_s.shape[0]
    reps = SGU_CHUNK // length
    tri = jnp.tril(jnp.ones((length, length), bool))
    ws = jnp.where(tri[None], w_s[:, :length, :length], 0.0)
    eye = jnp.eye(reps, dtype=ws.dtype)
    mats = (eye[None, :, None, :, None] * ws[:, None, :, None, :]).reshape(g, SGU_CHUNK, SGU_CHUNK)
    bias = jnp.tile(b_s[:, :length], (1, reps))
    bias = jnp.repeat(bias.T, SGU_GW, axis=1)
    return mats, bias


def kernel(x_prompt, x_sample, state_pool, cache_k, cache_v, g_mix, w_in, pool_w, pool_scale, sgu_norm, sgu_w, sgu_b, q_norm, k_norm, rel_table, w_gate, b_gate, w_br_a, w_br_b, w_br_c, w_out, g_ffn, peer_wq, peer_subkeys, peer_u, peer_v):
    depth = g_mix.shape[0]
    bsz, seq, d = x_prompt.shape
    dbsz, dseq, _ = x_sample.shape
    n_p, n_s = bsz * seq, dbsz * dseq
    n_cache = cache_k.shape[2]
    assert seq % ATT_QB == 0 and n_p % ROW_TILE == 0 and n_s % ROW_TILE == 0
    assert dseq <= SGU_CHUNK and SGU_CHUNK % dseq == 0 and dseq >= POOL_STATE + 1 and dseq % SUBLANES == 0
    assert n_cache == 2 * ATT_QB and dseq <= ATT_QB
    n_pad = -(n_p + n_s) % PEER_TB
    pad_rows = lambda w, dt: [jnp.zeros((n_pad, w), dt)] if n_pad else []
    x = jnp.concatenate([x_prompt.reshape(n_p, d), x_sample.reshape(n_s, d)] + pad_rows(d, F32), axis=0)

    hsum = _block_diag(jnp.full((ATT_HEADS, HEAD_DIM, HEAD_DIM), 1.0 / HEAD_DIM, F32)).astype(BF16)
    zero_prefix = jnp.zeros((bsz, HALO, POOL_W), F32)
    keep = min(BAND, seq)
    outs = {name: [] for name in ("pool_p", "pool_s", "k_p", "v_p", "k_s", "v_s", "sgu_s")}

    for l in range(depth):
        row = lambda a: a[l].reshape(1, -1)
        m_p, b_p = _sgu_mats(sgu_w[l], sgu_b[l], SGU_CHUNK)
        m_s, b_s = _sgu_mats(sgu_w[l], sgu_b[l], dseq)
        ms = jnp.stack([m_p, m_s]).astype(BF16)
        bs = jnp.stack([b_p, b_s])
        a, yb, vn, q, k, v = _in_proj(
            x, row(g_mix), w_in[l].astype(BF16), row(sgu_norm),
            jnp.tile(q_norm[l], ATT_HEADS).reshape(1, -1), jnp.tile(k_norm[l], ATT_HEADS).reshape(1, -1),
            hsum, ms, bs, n_p // ROW_TILE)

        pw = _block_diag(pool_w[l]).astype(BF16)
        sample_prefix = jnp.pad(state_pool[l], ((0, 0), (HALO - POOL_STATE, 0), (0, 0)))
        ya_p = _pool(a, zero_prefix, pw, row(pool_scale), 0, bsz, seq, ROW_TILE)
        ya_s = _pool(a, sample_prefix, pw, row(pool_scale), n_p, dbsz, dseq, dseq)

        bias_full, bias_band = _rel_bias(rel_table[l], ATT_QB, 3 * ATT_QB)
        yc_p = _attn_prompt(q, k, v, bias_band, bsz, seq)
        yc_s = _attn_sample(q, k, v, cache_k[l].reshape(dbsz, n_cache, ATT_W),
                            cache_v[l].reshape(dbsz, n_cache, ATT_W),
                            bias_full[:, :dseq, :n_cache + dseq], n_p, dbsz, dseq)

        h, hn, st = _merge(
            x, jnp.concatenate([ya_p, ya_s] + pad_rows(POOL_W, BF16), axis=0), yb,
            jnp.concatenate([yc_p, yc_s] + pad_rows(ATT_W, BF16), axis=0),
            row(g_mix), w_gate[l].astype(BF16), row(b_gate), w_br_a[l].astype(BF16),
            w_br_b[l].astype(BF16), w_br_c[l].astype(BF16), w_out[l].astype(BF16), row(g_ffn),
            peer_wq[l].astype(BF16), peer_subkeys[l].astype(BF16))
        isel, jsel, gsel = _topk(st)
        x = _peer(hn, h, isel, jsel, gsel, jnp.stack([peer_u[l].astype(BF16), peer_v[l].astype(BF16)]))

        def tail(arr, count):
            return jnp.stack([arr[(b + 1) * seq - count:(b + 1) * seq] for b in range(bsz)])

        sample = lambda arr: arr[n_p:n_p + n_s]
        outs["pool_p"].append(tail(a, POOL_STATE))
        outs["pool_s"].append(sample(a).reshape(dbsz, dseq, POOL_W)[:, dseq - POOL_STATE:])
        outs["k_p"].append(tail(k, keep).reshape(bsz, keep, ATT_HEADS, HEAD_DIM))
        outs["v_p"].append(tail(v, keep).reshape(bsz, keep, ATT_HEADS, HEAD_DIM))
        outs["k_s"].append(sample(k).reshape(dbsz, dseq, ATT_HEADS, HEAD_DIM))
        outs["v_s"].append(sample(v).reshape(dbsz, dseq, ATT_HEADS, HEAD_DIM))
        outs["sgu_s"].append(sample(vn).reshape(dbsz, dseq, SGU_W))

    st = lambda name: jnp.stack(outs[name])
    return (x[:n_p].reshape(bsz, seq, d), x[n_p:n_p + n_s].reshape(dbsz, dseq, d), st("pool_p"), st("pool_s"),
            st("k_p"), st("v_p"), st("k_s"), st("v_s"), st("sgu_s"))
```

```python
import functools

import jax
import jax.numpy as jnp
from jax import lax
from jax.experimental import pallas as pl
from jax.experimental.pallas import tpu as pltpu

F32 = jnp.float32
BF16 = jnp.bfloat16

EPS = 1e-6
D_MODEL = 1024
POOL_W = 256
POOL_GW = 64
POOL_STATE = 15
SGU_W = 256
SGU_GW = 64
SGU_CHUNK = 128
ATT_HEADS = 8
HEAD_DIM = 64
ATT_W = ATT_HEADS * HEAD_DIM
CHUNK = 64
BAND = 512
REL_CLIP = 128
N_BRANCH = 3
PEER_HEADS = 8
N_KEYS = 128
PK_HALF = 128
PK_TOPK = 16
NEG_BIG = -1e30

LANES = 128
SUBLANES = 8
ROW_TILE = 256
ATT_QB = 256
HALO = 16
TOPK_TT = 128
PEER_SLAB_ROWS = SUBLANES
PEER_N_SLABS = N_KEYS // PEER_SLAB_ROWS
PEER_SLABS_PER_STEP = 2
PEER_EC = PEER_SLABS_PER_STEP * PEER_SLAB_ROWS * N_KEYS
PEER_NC = N_KEYS * N_KEYS // PEER_EC
PEER_TOKEN_UNROLL = 8
PEER_TB = 512
VMEM_LIMIT = 56 * 1024 * 1024
PEER_VMEM_LIMIT = 60 * 1024 * 1024


def _cparams(sem):
    return pltpu.CompilerParams(dimension_semantics=sem, vmem_limit_bytes=VMEM_LIMIT)


def _const_spec(shape):
    n = len(shape)
    return pl.BlockSpec(shape, lambda *_: (0,) * n)


def _nt_dot(a, b):
    return lax.dot_general(a, b, (((1,), (1,)), ((), ())), preferred_element_type=F32)


def _split_bf16(x):
    hi = x.astype(BF16)
    lo = (x - hi.astype(F32)).astype(BF16)
    return hi, lo


def _in_proj_kernel(x_ref, gmix_ref, win_ref, sgn_ref, qn_ref, kn_ref, hsum_ref, ms_ref, bs_ref,
                    a_ref, yb_ref, vn_ref, q_ref, k_ref, v_ref):
    x = x_ref[...]
    xn = x * lax.rsqrt(jnp.mean(x * x, axis=-1, keepdims=True) + EPS) * gmix_ref[...]
    z = jnp.dot(xn.astype(BF16), win_ref[...], preferred_element_type=F32)
    a_ref[...] = z[:, 0:POOL_W]
    u = z[:, POOL_W:POOL_W + SGU_W]
    vb = z[:, POOL_W + SGU_W:POOL_W + 2 * SGU_W]
    vn = vb * lax.rsqrt(jnp.mean(vb * vb, axis=-1, keepdims=True) + EPS) * sgn_ref[...]
    vn_ref[...] = vn
    vnb = vn.astype(BF16)
    lane_group = lax.broadcasted_iota(jnp.int32, (SGU_CHUNK, SGU_W), 1) // SGU_GW
    rows = x.shape[0]
    for c in range(rows // SGU_CHUNK):
        sl = slice(c * SGU_CHUNK, (c + 1) * SGU_CHUNK)
        vc = vnb[sl, :]
        mixed = jnp.dot(ms_ref[0, 0], vc, preferred_element_type=F32)
        for g in range(1, SGU_W // SGU_GW):
            mg = jnp.dot(ms_ref[0, g], vc, preferred_element_type=F32)
            mixed = jnp.where(lane_group == g, mg, mixed)
        yb_ref[sl, :] = (u[sl, :] * (mixed + bs_ref[0])).astype(BF16)

    base = POOL_W + 2 * SGU_W
    q = z[:, base:base + ATT_W]
    k = z[:, base + ATT_W:base + 2 * ATT_W]
    v_ref[...] = z[:, base + 2 * ATT_W:base + 3 * ATT_W]

    def head_norm(t, w_ref):
        hi, lo = _split_bf16(t * t)
        m = (jnp.dot(hi, hsum_ref[...], preferred_element_type=F32)
             + jnp.dot(lo, hsum_ref[...], preferred_element_type=F32))
        return t * lax.rsqrt(m + EPS) * w_ref[...]

    q_ref[...] = (head_norm(q, qn_ref) * (HEAD_DIM ** -0.5)).astype(BF16)
    k_ref[...] = head_norm(k, kn_ref)


def _in_proj(x, gmix, win, sgn, qn, kn, hsum, ms, bs, n_prompt_tiles):
    n = x.shape[0]
    tm = ROW_TILE
    in_w = win.shape[1]
    row = lambda w: pl.BlockSpec((tm, w), lambda i: (i, 0))
    sel = lambda i: (jnp.where(i >= n_prompt_tiles, 1, 0), 0, 0, 0)
    sel3 = lambda i: (jnp.where(i >= n_prompt_tiles, 1, 0), 0, 0)
    return pl.pallas_call(
        _in_proj_kernel,
        grid=(n // tm,),
        in_specs=[row(D_MODEL), _const_spec((1, D_MODEL)), _const_spec((D_MODEL, in_w)),
                  _const_spec((1, SGU_W)), _const_spec((1, ATT_W)), _const_spec((1, ATT_W)),
                  _const_spec((ATT_W, ATT_W)),
                  pl.BlockSpec((1, SGU_W // SGU_GW, SGU_CHUNK, SGU_CHUNK), sel),
                  pl.BlockSpec((1, SGU_CHUNK, SGU_W), sel3)],
        out_specs=[row(POOL_W), row(SGU_W), row(SGU_W), row(ATT_W), row(ATT_W), row(ATT_W)],
        out_shape=[jax.ShapeDtypeStruct((n, POOL_W), F32), jax.ShapeDtypeStruct((n, SGU_W), BF16),
                   jax.ShapeDtypeStruct((n, SGU_W), F32), jax.ShapeDtypeStruct((n, ATT_W), BF16),
                   jax.ShapeDtypeStruct((n, ATT_W), F32), jax.ShapeDtypeStruct((n, ATT_W), F32)],
        compiler_params=_cparams(("arbitrary",)),
        name="in_proj",
    )(x, gmix, win, sgn, qn, kn, hsum, ms, bs)


def _pool_kernel(a_ref, prev_ref, pre_ref, pw_ref, sc_ref, y_ref):
    t = pl.program_id(1)
    a = a_ref[...]
    tm = a.shape[0]
    halo = jnp.where(t == 0, pre_ref[0], prev_ref[...])
    e = jnp.concatenate([halo, a], axis=0)
    s2 = e[1:] + e[:-1]
    s4 = s2[2:] + s2[:-2]
    s8 = s4[4:] + s4[:-4]
    s16 = s8[8:] + s8[:-8]
    lg = lax.broadcasted_iota(jnp.int32, (tm, POOL_W), 1) // POOL_GW
    mean = jnp.where(lg == 0, s2[15:] * 0.5,
                     jnp.where(lg == 1, s4[13:] * 0.25,
                               jnp.where(lg == 2, s8[9:] * 0.125, s16[1:] * 0.0625)))
    pooled = mean - a
    y = jnp.dot(pooled.astype(BF16), pw_ref[...], preferred_element_type=F32) * sc_ref[...]
    y_ref[...] = y.astype(BF16)


def _pool(a, prefix, pw, sc, row0, n_streams, t_len, tm):
    nt = t_len // tm
    b0 = row0 // tm
    hb = tm // HALO
    return pl.pallas_call(
        _pool_kernel,
        grid=(n_streams, nt),
        in_specs=[pl.BlockSpec((tm, POOL_W), lambda b, t: (b0 + b * nt + t, 0)),
                  pl.BlockSpec((HALO, POOL_W), lambda b, t: (jnp.maximum((b0 + b * nt + t) * hb - 1, 0), 0)),
                  pl.BlockSpec((1, HALO, POOL_W), lambda b, t: (b, 0, 0)),
                  _const_spec((POOL_W, POOL_W)), _const_spec((1, POOL_W))],
        out_specs=pl.BlockSpec((tm, POOL_W), lambda b, t: (b * nt + t, 0)),
        out_shape=jax.ShapeDtypeStruct((n_streams * t_len, POOL_W), BF16),
        compiler_params=_cparams(("arbitrary", "arbitrary")),
        name="pool_mix",
    )(a, a, prefix, pw, sc)


def _attend(q, k, v, bias_ref, col_bias, o_ref):
    qb = q.shape[0]
    lane_hi = lax.broadcasted_iota(jnp.int32, (qb, 2 * HEAD_DIM), 1) >= HEAD_DIM
    for hp in range(ATT_HEADS // 2):
        sl = slice(hp * 2 * HEAD_DIM, (hp + 1) * 2 * HEAD_DIM)
        q2, k2, v2 = q[:, sl], k[:, sl], v[:, sl]
        out = None
        for sub in range(2):
            qm = jnp.where(lane_hi == (sub == 1), q2, jnp.zeros_like(q2))
            s = _nt_dot(qm, k2) + bias_ref[hp * 2 + sub]
            if col_bias is not None:
                s = s + col_bias
            m = jnp.max(s, axis=-1, keepdims=True)
            p = jnp.exp(s - m)
            l = jnp.sum(p, axis=-1, keepdims=True)
            o = jnp.dot(p.astype(BF16), v2, preferred_element_type=F32) / l
            out = o if sub == 0 else jnp.where(lane_hi, o, out)
        o_ref[:, sl] = out.astype(BF16)


def _attn_prompt_kernel(q_ref, k0_ref, k1_ref, k2_ref, v0_ref, v1_ref, v2_ref, bias_ref, o_ref):
    t = pl.program_id(1)
    qb = q_ref.shape[0]
    k = jnp.concatenate([k0_ref[...], k1_ref[...], k2_ref[...]], axis=0).astype(BF16)
    v = jnp.concatenate([v0_ref[...], v1_ref[...], v2_ref[...]], axis=0).astype(BF16)
    key_row = (t - 2) * qb + lax.broadcasted_iota(jnp.int32, (1, 3 * qb), 1)
    col_bias = jnp.where(key_row >= 0, 0.0, NEG_BIG).astype(F32)
    _attend(q_ref[...], k, v, bias_ref, col_bias, o_ref)


def _attn_prompt(q, k, v, bias, n_streams, t_len):
    qb = ATT_QB
    nt = t_len // qb
    cur = lambda b, t: (b * nt + t, 0)
    prev1 = lambda b, t: (b * nt + jnp.maximum(t - 1, 0), 0)
    prev2 = lambda b, t: (b * nt + jnp.maximum(t - 2, 0), 0)
    blk = lambda im: pl.BlockSpec((qb, ATT_W), im)
    return pl.pallas_call(
        _attn_prompt_kernel,
        grid=(n_streams, nt),
        in_specs=[blk(cur), blk(prev2), blk(prev1), blk(cur), blk(prev2), blk(prev1), blk(cur),
                  _const_spec(bias.shape)],
        out_specs=blk(cur),
        out_shape=jax.ShapeDtypeStruct((n_streams * t_len, ATT_W), BF16),
        compiler_params=_cparams(("arbitrary", "arbitrary")),
        name="attn_prompt",
    )(q, k, k, k, v, v, v, bias)


def _attn_sample_kernel(q_ref, kn_ref, vn_ref, kc_ref, vc_ref, bias_ref, o_ref):
    k = jnp.concatenate([kc_ref[0], kn_ref[...]], axis=0).astype(BF16)
    v = jnp.concatenate([vc_ref[0], vn_ref[...]], axis=0).astype(BF16)
    _attend(q_ref[...], k, v, bias_ref, None, o_ref)


def _attn_sample(q, k, v, kc, vc, bias, row0, n_streams, s_len):
    b0 = row0 // s_len
    n_cache = kc.shape[1]
    new = pl.BlockSpec((s_len, ATT_W), lambda s: (b0 + s, 0))
    cache = pl.BlockSpec((1, n_cache, ATT_W), lambda s: (s, 0, 0))
    return pl.pallas_call(
        _attn_sample_kernel,
        grid=(n_streams,),
        in_specs=[new, new, new, cache, cache, _const_spec(bias.shape)],
        out_specs=pl.BlockSpec((s_len, ATT_W), lambda s: (s, 0)),
        out_shape=jax.ShapeDtypeStruct((n_streams * s_len, ATT_W), BF16),
        compiler_params=_cparams(("arbitrary",)),
        name="attn_sample",
    )(q, k, v, kc, vc, bias)


def _merge_kernel(x_ref, ya_ref, yb_ref, yc_ref, gmix_ref, wg_ref, bg_ref, wa_ref, wb_ref, wc_ref,
                  wo_ref, gffn_ref, wq_ref, sk_ref, h_ref, hn_ref, st_ref):
    x = x_ref[...]
    xn = (x * lax.rsqrt(jnp.mean(x * x, axis=-1, keepdims=True) + EPS) * gmix_ref[...]).astype(BF16)
    merged = None
    for b, (y_ref, w_ref) in enumerate(((ya_ref, wa_ref), (yb_ref, wb_ref), (yc_ref, wc_ref))):
        cols = slice(b * D_MODEL, (b + 1) * D_MODEL)
        gate = jax.nn.sigmoid(jnp.dot(xn, wg_ref[:, cols], preferred_element_type=F32) + bg_ref[:, cols])
        term = gate * jnp.dot(y_ref[...], w_ref[...], preferred_element_type=F32)
        merged = term if merged is None else merged + term
    h = x + jnp.dot(merged.astype(BF16), wo_ref[...], preferred_element_type=F32)
    h_ref[...] = h
    hn = (h * lax.rsqrt(jnp.mean(h * h, axis=-1, keepdims=True) + EPS) * gffn_ref[...]).astype(BF16)
    hn_ref[...] = hn
    qp = jnp.dot(hn, wq_ref[...], preferred_element_type=F32).astype(BF16)
    for hp in range(2 * PEER_HEADS):
        st_ref[hp] = _nt_dot(sk_ref[hp % 2], qp[:, hp * PK_HALF:(hp + 1) * PK_HALF])


def _merge(x, ya, yb, yc, gmix, wg, bg, wa, wb, wc, wo, gffn, wq, sk):
    n = x.shape[0]
    tm = ROW_TILE
    row = lambda w: pl.BlockSpec((tm, w), lambda i: (i, 0))
    nq = wq.shape[1]
    return pl.pallas_call(
        _merge_kernel,
        grid=(n // tm,),
        in_specs=[row(D_MODEL), row(POOL_W), row(SGU_W), row(ATT_W),
                  _const_spec((1, D_MODEL)), _const_spec(wg.shape), _const_spec(bg.shape),
                  _const_spec(wa.shape), _const_spec(wb.shape), _const_spec(wc.shape),
                  _const_spec(wo.shape), _const_spec((1, D_MODEL)), _const_spec(wq.shape),
                  _const_spec(sk.shape)],
        out_specs=[row(D_MODEL), row(D_MODEL),
                   pl.BlockSpec((2 * PEER_HEADS, N_KEYS, tm), lambda i: (0, 0, i))],
        out_shape=[jax.ShapeDtypeStruct((n, D_MODEL), F32), jax.ShapeDtypeStruct((n, D_MODEL), BF16),
                   jax.ShapeDtypeStruct((2 * PEER_HEADS, N_KEYS, n), F32)],
        compiler_params=_cparams(("arbitrary",)),
        name="merge",
    )(x, ya, yb, yc, gmix, wg, bg, wa, wb, wc, wo, gffn, wq, sk)


def _top16(x, iota, payloads=()):
    n = x.shape[0]
    vals, idxs = [], []
    picked = [[] for _ in payloads]
    for _ in range(PK_TOPK):
        m = jnp.max(x, axis=0, keepdims=True)
        idx = jnp.min(jnp.where(x == m, iota, float(n)), axis=0, keepdims=True)
        hit = iota == idx
        vals.append(m)
        idxs.append(idx)
        for lst, p in zip(picked, payloads):
            lst.append(jnp.sum(jnp.where(hit, p, 0.0), axis=0, keepdims=True))
        x = jnp.where(hit, -jnp.inf, x)
    return vals, idxs, picked


_PAIRS = [(a, b) for a in range(PK_TOPK) for b in range(PK_TOPK) if (a + 1) * (b + 1) <= PK_TOPK]
_PAIR_ROWS = -(-len(_PAIRS) // SUBLANES) * SUBLANES


def _topk_kernel(st_ref, i_ref, j_ref, g_ref):
    tt = st_ref.shape[2]
    iota_k = lax.broadcasted_iota(jnp.int32, (N_KEYS, tt), 0).astype(F32)
    iota_c = lax.broadcasted_iota(jnp.int32, (_PAIR_ROWS, tt), 0).astype(F32)
    pad = _PAIR_ROWS - len(_PAIRS)
    neg_row = jnp.full((1, tt), -jnp.inf, F32)
    zero_row = jnp.zeros((1, tt), F32)
    i_rows, j_rows, g_rows = [], [], []
    for h in range(PEER_HEADS):
        v0, i0, _ = _top16(st_ref[2 * h], iota_k)
        v1, i1, _ = _top16(st_ref[2 * h + 1], iota_k)
        cand = jnp.concatenate([v0[a] + v1[b] for a, b in _PAIRS] + [neg_row] * pad, axis=0)
        cand_i = jnp.concatenate([i0[a] for a, _ in _PAIRS] + [zero_row] * pad, axis=0)
        cand_j = jnp.concatenate([i1[b] for _, b in _PAIRS] + [zero_row] * pad, axis=0)
        tv, _, (ti, tj) = _top16(cand, iota_c, (cand_i, cand_j))
        e = jnp.exp(jnp.concatenate(tv, axis=0) - tv[0])
        g_rows.append(e / jnp.sum(e, axis=0, keepdims=True))
        i_rows += ti
        j_rows += tj
    i_ref[...] = jnp.concatenate(i_rows, axis=0).T
    j_ref[...] = jnp.concatenate(j_rows, axis=0).T
    g_ref[...] = jnp.concatenate(g_rows, axis=0).T


def _topk(st):
    n = st.shape[2]
    tt = TOPK_TT
    nsel = PEER_HEADS * PK_TOPK
    out = pl.BlockSpec((tt, nsel), lambda i: (i, 0))
    return pl.pallas_call(
        _topk_kernel,
        grid=(n // tt,),
        in_specs=[pl.BlockSpec((2 * PEER_HEADS, N_KEYS, tt), lambda i: (0, 0, i))],
        out_specs=[out, out, out],
        out_shape=[jax.ShapeDtypeStruct((n, nsel), F32)] * 3,
        compiler_params=_cparams(("arbitrary",)),
        name="peer_topk",
    )(st)


def _gelu(x):
    return 0.5 * x * (1.0 + lax.erf(x * (2.0 ** -0.5)))


def _peer_kernel(hn_ref, h_ref, i_ref, j_ref, g_ref, uv_ref, o_ref, hs_ref, gm_ref):
    s = pl.program_id(1)
    tb = hn_ref.shape[0]
    rpc = PEER_SLAB_ROWS
    sps = PEER_SLABS_PER_STEP

    @pl.when(s < PEER_NC)
    def _scores():
        hc = _nt_dot(hn_ref[...], uv_ref[...])
        for k in range(sps):
            slab = hs_ref.at[s * sps + k]
            for ib in range(rpc):
                col = (k * rpc + ib) * N_KEYS
                slab[pl.ds(ib, tb, stride=rpc), :] = hc[:, col:col + N_KEYS]

    @pl.when(s == PEER_NC - 1)
    def _weights():
        sub = lax.broadcasted_iota(jnp.int32, (N_KEYS, N_KEYS), 0).astype(F32).astype(BF16)
        one = jnp.ones((N_KEYS, N_KEYS), BF16)
        zero = jnp.zeros((N_KEYS, N_KEYS), BF16)

        def gate_matrix(n):
            irow = i_ref[pl.ds(n, 1), :].astype(BF16)
            jrow = j_ref[pl.ds(n, 1), :].astype(BF16)
            g_hi, g_lo = _split_bf16(0.5 * g_ref[pl.ds(n, 1), :])
            at = jnp.where(sub == irow, one, zero)
            jhit = sub == jrow
            b_hi = jnp.where(jhit, jnp.broadcast_to(g_hi, jhit.shape), zero)
            b_lo = jnp.where(jhit, jnp.broadcast_to(g_lo, jhit.shape), zero)
            return _nt_dot(jnp.concatenate([at, at], axis=1),
                           jnp.concatenate([b_hi, b_lo], axis=1))

        unroll = PEER_TOKEN_UNROLL
        n_groups = tb // unroll

        def make_gates(group, slot):
            group = jnp.minimum(group, n_groups - 1)
            for t in range(unroll):
                gm_ref[slot, t * N_KEYS:(t + 1) * N_KEYS, :] = gate_matrix(group * unroll + t)

        def apply_gates(group, slot):
            base = pl.multiple_of(group * (unroll * rpc), unroll * rpc)
            for c in range(PEER_N_SLABS):
                hv = hs_ref[c, pl.ds(base, unroll * rpc), :]
                gm = jnp.concatenate([gm_ref[slot, t * N_KEYS + c * rpc:t * N_KEYS + (c + 1) * rpc, :]
                                      for t in range(unroll)], axis=0)
                hs_ref[c, pl.ds(base, unroll * rpc), :] = (hv * gm) * (1.0 + lax.erf(hv * (2.0 ** -0.5)))

        def body(m, carry):
            apply_gates(2 * m, 0)
            make_gates(2 * m + 1, 1)
            apply_gates(2 * m + 1, 1)
            make_gates(2 * m + 2, 0)
            return carry

        o_ref[...] = h_ref[...]
        make_gates(0, 0)
        lax.fori_loop(0, n_groups // 2, body, 0)

    @pl.when(s >= PEER_NC)
    def _combine():
        c = s - PEER_NC
        wc = jnp.concatenate([hs_ref.at[c * sps + k][pl.ds(ib, tb, stride=rpc), :]
                              for k in range(sps) for ib in range(rpc)], axis=1)
        o_ref[...] += jnp.dot(wc.astype(BF16), uv_ref[...], preferred_element_type=F32)


def _peer(hn, h, isel, jsel, gsel, uv):
    n = hn.shape[0]
    tb = PEER_TB
    nsel = PEER_HEADS * PK_TOPK
    nc = PEER_NC
    row = lambda w: pl.BlockSpec((tb, w), lambda b, s: (b, 0))
    return pl.pallas_call(
        _peer_kernel,
        grid=(n // tb, 2 * nc),
        in_specs=[row(D_MODEL), row(D_MODEL), row(nsel), row(nsel), row(nsel),
                  pl.BlockSpec((None, PEER_EC, D_MODEL), lambda b, s: (s // nc, s % nc, 0))],
        out_specs=row(D_MODEL),
        out_shape=jax.ShapeDtypeStruct((n, D_MODEL), F32),
        scratch_shapes=[pltpu.VMEM((PEER_N_SLABS, tb * PEER_SLAB_ROWS, N_KEYS), F32),
                        pltpu.VMEM((2, PEER_TOKEN_UNROLL * N_KEYS, N_KEYS), F32)],
        compiler_params=pltpu.CompilerParams(dimension_semantics=("arbitrary", "arbitrary"),
                                             vmem_limit_bytes=PEER_VMEM_LIMIT),
        name="peer_experts",
    )(hn, h, isel, jsel, gsel, uv)


def _bias_kernel(g_ref, full_ref, band_ref):
    _, nq, nk = full_ref.shape
    w = g_ref.shape[1]
    qc = lax.broadcasted_iota(jnp.int32, (nq, nk), 0) // CHUNK
    kc = lax.broadcasted_iota(jnp.int32, (nq, nk), 1) // CHUNK
    band = jnp.abs(2 * (kc - qc) - BAND // CHUNK) <= BAND // CHUNK
    for h in range(ATT_HEADS):
        line = jnp.broadcast_to(g_ref[h:h + 1, :], (nq, w))
        t = pltpu.roll(line, w - nq + 1, 1, stride=1, stride_axis=0)[:, :nk]
        full_ref[h] = t
        band_ref[h] = jnp.where(band, t, NEG_BIG)


def _rel_bias(table, nq, nk):
    w = 1024
    assert nq + nk - 1 <= w
    dist = nk - 1 - jnp.arange(w)
    line = table[:, jnp.clip(dist, -REL_CLIP, REL_CLIP) + REL_CLIP].astype(F32)
    shape = jax.ShapeDtypeStruct((ATT_HEADS, nq, nk), F32)
    return pl.pallas_call(
        _bias_kernel,
        out_shape=[shape, shape],
        compiler_params=pltpu.CompilerParams(vmem_limit_bytes=VMEM_LIMIT),
        name="rel_bias",
    )(line)


def _block_diag(blocks):
    g, r, c = blocks.shape
    eye = jnp.eye(g, dtype=blocks.dtype)
    return (eye[:, None, :, None] * blocks[:, :, None, :]).reshape(g * r, g * c)


def _sgu_mats(w_s, b_s, length):
    g = w_s.shape[0]
    reps = SGU_CHUNK // length
    tri = jnp.tril(jnp.ones((length, length), bool))
    ws = jnp.where(tri[None], w_s[:, :length, :length], 0.0)
    eye = jnp.eye(reps, dtype=ws.dtype)
    mats = (eye[None, :, None, :, None] * ws[:, None, :, None, :]).reshape(g, SGU_CHUNK, SGU_CHUNK)
    bias = jnp.tile(b_s[:, :length], (1, reps))
    bias = jnp.repeat(bias.T, SGU_GW, axis=1)
    return mats, bias


def kernel(x_prompt, x_sample, state_pool, cache_k, cache_v, g_mix, w_in, pool_w, pool_scale, sgu_norm, sgu_w, sgu_b, q_norm, k_norm, rel_table, w_gate, b_gate, w_br_a, w_br_b, w_br_c, w_out, g_ffn, peer_wq, peer_subkeys, peer_u, peer_v):
    depth = g_mix.shape[0]
    bsz, seq, d = x_prompt.shape
    dbsz, dseq, _ = x_sample.shape
    n_p, n_s = bsz * seq, dbsz * dseq
    n_cache = cache_k.shape[2]
    assert seq % ATT_QB == 0 and n_p % ROW_TILE == 0 and n_s % ROW_TILE == 0
    assert dseq <= SGU_CHUNK and SGU_CHUNK % dseq == 0 and dseq >= POOL_STATE + 1 and dseq % SUBLANES == 0
    assert n_cache == 2 * ATT_QB and dseq <= ATT_QB
    n_pad = -(n_p + n_s) % PEER_TB
    pad_rows = lambda w, dt: [jnp.zeros((n_pad, w), dt)] if n_pad else []
    x = jnp.concatenate([x_prompt.reshape(n_p, d), x_sample.reshape(n_s, d)] + pad_rows(d, F32), axis=0)

    hsum = _block_diag(jnp.full((ATT_HEADS, HEAD_DIM, HEAD_DIM), 1.0 / HEAD_DIM, F32)).astype(BF16)
    zero_prefix = jnp.zeros((bsz, HALO, POOL_W), F32)
    keep = min(BAND, seq)
    outs = {name: [] for name in ("pool_p", "pool_s", "k_p", "v_p", "k_s", "v_s", "sgu_s")}

    for l in range(depth):
        row = lambda a: a[l].reshape(1, -1)
        m_p, b_p = _sgu_mats(sgu_w[l], sgu_b[l], SGU_CHUNK)
        m_s, b_s = _sgu_mats(sgu_w[l], sgu_b[l], dseq)
        ms = jnp.stack([m_p, m_s]).astype(BF16)
        bs = jnp.stack([b_p, b_s])
        a, yb, vn, q, k, v = _in_proj(
            x, row(g_mix), w_in[l].astype(BF16), row(sgu_norm),
            jnp.tile(q_norm[l], ATT_HEADS).reshape(1, -1), jnp.tile(k_norm[l], ATT_HEADS).reshape(1, -1),
            hsum, ms, bs, n_p // ROW_TILE)

        pw = _block_diag(pool_w[l]).astype(BF16)
        sample_prefix = jnp.pad(state_pool[l], ((0, 0), (HALO - POOL_STATE, 0), (0, 0)))
        ya_p = _pool(a, zero_prefix, pw, row(pool_scale), 0, bsz, seq, ROW_TILE)
        ya_s = _pool(a, sample_prefix, pw, row(pool_scale), n_p, dbsz, dseq, dseq)

        bias_full, bias_band = _rel_bias(rel_table[l], ATT_QB, 3 * ATT_QB)
        yc_p = _attn_prompt(q, k, v, bias_band, bsz, seq)
        yc_s = _attn_sample(q, k, v, cache_k[l].reshape(dbsz, n_cache, ATT_W),
                            cache_v[l].reshape(dbsz, n_cache, ATT_W),
                            bias_full[:, :dseq, :n_cache + dseq], n_p, dbsz, dseq)

        h, hn, st = _merge(
            x, jnp.concatenate([ya_p, ya_s] + pad_rows(POOL_W, BF16), axis=0), yb,
            jnp.concatenate([yc_p, yc_s] + pad_rows(ATT_W, BF16), axis=0),
            row(g_mix), w_gate[l].astype(BF16), row(b_gate), w_br_a[l].astype(BF16),
            w_br_b[l].astype(BF16), w_br_c[l].astype(BF16), w_out[l].astype(BF16), row(g_ffn),
            peer_wq[l].astype(BF16), peer_subkeys[l].astype(BF16))
        isel, jsel, gsel = _topk(st)
        x = _peer(hn, h, isel, jsel, gsel, jnp.stack([peer_u[l].astype(BF16), peer_v[l].astype(BF16)]))

        def tail(arr, count):
            return jnp.stack([arr[(b + 1) * seq - count:(b + 1) * seq] for b in range(bsz)])

        sample = lambda arr: arr[n_p:n_p + n_s]
        outs["pool_p"].append(tail(a, POOL_STATE))
        outs["pool_s"].append(sample(a).reshape(dbsz, dseq, POOL_W)[:, dseq - POOL_STATE:])
        outs["k_p"].append(tail(k, keep).reshape(bsz, keep, ATT_HEADS, HEAD_DIM))
        outs["v_p"].append(tail(v, keep).reshape(bsz, keep, ATT_HEADS, HEAD_DIM))
        outs["k_s"].append(sample(k).reshape(dbsz, dseq, ATT_HEADS, HEAD_DIM))
        outs["v_s"].append(sample(v).reshape(dbsz, dseq, ATT_HEADS, HEAD_DIM))
        outs["sgu_s"].append(sample(vn).reshape(dbsz, dseq, SGU_W))

    st = lambda name: jnp.stack(outs[name])
    return (x[:n_p].reshape(bsz, seq, d), x[n_p:n_p + n_s].reshape(dbsz, dseq, d), st("pool_p"), st("pool_s"),
            st("k_p"), st("v_p"), st("k_s"), st("v_s"), st("sgu_s"))
```

```python
import functools

import jax
import jax.numpy as jnp
from jax import lax
from jax.experimental import pallas as pl
from jax.experimental.pallas import tpu as pltpu

F32 = jnp.float32
BF16 = jnp.bfloat16

EPS = 1e-6
D_MODEL = 1024
POOL_W = 256
POOL_GW = 64
POOL_STATE = 15
SGU_W = 256
SGU_GW = 64
SGU_CHUNK = 128
ATT_HEADS = 8
HEAD_DIM = 64
ATT_W = ATT_HEADS * HEAD_DIM
CHUNK = 64
BAND = 512
REL_CLIP = 128
N_BRANCH = 3
PEER_HEADS = 8
N_KEYS = 128
PK_HALF = 128
PK_TOPK = 16
NEG_BIG = -1e30

LANES = 128
SUBLANES = 8
ROW_TILE = 256
ATT_QB = 256
HALO = 16
TOPK_TT = 128
PEER_SLAB_ROWS = SUBLANES
PEER_N_SLABS = N_KEYS // PEER_SLAB_ROWS
PEER_SLABS_PER_STEP = 2
PEER_EC = PEER_SLABS_PER_STEP * PEER_SLAB_ROWS * N_KEYS
PEER_NC = N_KEYS * N_KEYS // PEER_EC
PEER_TOKEN_UNROLL = 8
PEER_TB = 512
VMEM_LIMIT = 56 * 1024 * 1024
PEER_VMEM_LIMIT = 60 * 1024 * 1024


def _cparams(sem):
    return pltpu.CompilerParams(dimension_semantics=sem, vmem_limit_bytes=VMEM_LIMIT)


def _const_spec(shape):
    n = len(shape)
    return pl.BlockSpec(shape, lambda *_: (0,) * n)


def _nt_dot(a, b):
    return lax.dot_general(a, b, (((1,), (1,)), ((), ())), preferred_element_type=F32)


def _split_bf16(x):
    hi = x.astype(BF16)
    lo = (x - hi.astype(F32)).astype(BF16)
    return hi, lo


def _in_proj_kernel(x_ref, gmix_ref, win_ref, sgn_ref, qn_ref, kn_ref, hsum_ref, ms_ref, bs_ref,
                    a_ref, yb_ref, vn_ref, q_ref, k_ref, v_ref):
    x = x_ref[...]
    xn = x * lax.rsqrt(jnp.mean(x * x, axis=-1, keepdims=True) + EPS) * gmix_ref[...]
    z = jnp.dot(xn.astype(BF16), win_ref[...], preferred_element_type=F32)
    a_ref[...] = z[:, 0:POOL_W]
    u = z[:, POOL_W:POOL_W + SGU_W]
    vb = z[:, POOL_W + SGU_W:POOL_W + 2 * SGU_W]
    vn = vb * lax.rsqrt(jnp.mean(vb * vb, axis=-1, keepdims=True) + EPS) * sgn_ref[...]
    vn_ref[...] = vn
    vnb = vn.astype(BF16)
    lane_group = lax.broadcasted_iota(jnp.int32, (SGU_CHUNK, SGU_W), 1) // SGU_GW
    rows = x.shape[0]
    for c in range(rows // SGU_CHUNK):
        sl = slice(c * SGU_CHUNK, (c + 1) * SGU_CHUNK)
        vc = vnb[sl, :]
        mixed = jnp.dot(ms_ref[0, 0], vc, preferred_element_type=F32)
        for g in range(1, SGU_W // SGU_GW):
            mg = jnp.dot(ms_ref[0, g], vc, preferred_element_type=F32)
            mixed = jnp.where(lane_group == g, mg, mixed)
        yb_ref[sl, :] = (u[sl, :] * (mixed + bs_ref[0])).astype(BF16)

    base = POOL_W + 2 * SGU_W
    q = z[:, base:base + ATT_W]
    k = z[:, base + ATT_W:base + 2 * ATT_W]
    v_ref[...] = z[:, base + 2 * ATT_W:base + 3 * ATT_W]

    def head_norm(t, w_ref):
        hi, lo = _split_bf16(t * t)
        m = (jnp.dot(hi, hsum_ref[...], preferred_element_type=F32)
             + jnp.dot(lo, hsum_ref[...], preferred_element_type=F32))
        return t * lax.rsqrt(m + EPS) * w_ref[...]

    q_ref[...] = (head_norm(q, qn_ref) * (HEAD_DIM ** -0.5)).astype(BF16)
    k_ref[...] = head_norm(k, kn_ref)


def _in_proj(x, gmix, win, sgn, qn, kn, hsum, ms, bs, n_prompt_tiles):
    n = x.shape[0]
    tm = ROW_TILE
    in_w = win.shape[1]
    row = lambda w: pl.BlockSpec((tm, w), lambda i: (i, 0))
    sel = lambda i: (jnp.where(i >= n_prompt_tiles, 1, 0), 0, 0, 0)
    sel3 = lambda i: (jnp.where(i >= n_prompt_tiles, 1, 0), 0, 0)
    return pl.pallas_call(
        _in_proj_kernel,
        grid=(n // tm,),
        in_specs=[row(D_MODEL), _const_spec((1, D_MODEL)), _const_spec((D_MODEL, in_w)),
                  _const_spec((1, SGU_W)), _const_spec((1, ATT_W)), _const_spec((1, ATT_W)),
                  _const_spec((ATT_W, ATT_W)),
                  pl.BlockSpec((1, SGU_W // SGU_GW, SGU_CHUNK, SGU_CHUNK), sel),
                  pl.BlockSpec((1, SGU_CHUNK, SGU_W), sel3)],
        out_specs=[row(POOL_W), row(SGU_W), row(SGU_W), row(ATT_W), row(ATT_W), row(ATT_W)],
        out_shape=[jax.ShapeDtypeStruct((n, POOL_W), F32), jax.ShapeDtypeStruct((n, SGU_W), BF16),
                   jax.ShapeDtypeStruct((n, SGU_W), F32), jax.ShapeDtypeStruct((n, ATT_W), BF16),
                   jax.ShapeDtypeStruct((n, ATT_W), F32), jax.ShapeDtypeStruct((n, ATT_W), F32)],
        compiler_params=_cparams(("arbitrary",)),
        name="in_proj",
    )(x, gmix, win, sgn, qn, kn, hsum, ms, bs)


def _pool_kernel(a_ref, prev_ref, pre_ref, pw_ref, sc_ref, y_ref):
    t = pl.program_id(1)
    a = a_ref[...]
    tm = a.shape[0]
    halo = jnp.where(t == 0, pre_ref[0], prev_ref[...])
    e = jnp.concatenate([halo, a], axis=0)
    s2 = e[1:] + e[:-1]
    s4 = s2[2:] + s2[:-2]
    s8 = s4[4:] + s4[:-4]
    s16 = s8[8:] + s8[:-8]
    lg = lax.broadcasted_iota(jnp.int32, (tm, POOL_W), 1) // POOL_GW
    mean = jnp.where(lg == 0, s2[15:] * 0.5,
                     jnp.where(lg == 1, s4[13:] * 0.25,
                               jnp.where(lg == 2, s8[9:] * 0.125, s16[1:] * 0.0625)))
    pooled = mean - a
    y = jnp.dot(pooled.astype(BF16), pw_ref[...], preferred_element_type=F32) * sc_ref[...]
    y_ref[...] = y.astype(BF16)


def _pool(a, prefix, pw, sc, row0, n_streams, t_len, tm):
    nt = t_len // tm
    b0 = row0 // tm
    hb = tm // HALO
    return pl.pallas_call(
        _pool_kernel,
        grid=(n_streams, nt),
        in_specs=[pl.BlockSpec((tm, POOL_W), lambda b, t: (b0 + b * nt + t, 0)),
                  pl.BlockSpec((HALO, POOL_W), lambda b, t: (jnp.maximum((b0 + b * nt + t) * hb - 1, 0), 0)),
                  pl.BlockSpec((1, HALO, POOL_W), lambda b, t: (b, 0, 0)),
                  _const_spec((POOL_W, POOL_W)), _const_spec((1, POOL_W))],
        out_specs=pl.BlockSpec((tm, POOL_W), lambda b, t: (b * nt + t, 0)),
        out_shape=jax.ShapeDtypeStruct((n_streams * t_len, POOL_W), BF16),
        compiler_params=_cparams(("arbitrary", "arbitrary")),
        name="pool_mix",
    )(a, a, prefix, pw, sc)


def _attend(q, k, v, bias_ref, col_bias, o_ref):
    qb = q.shape[0]
    lane_hi = lax.broadcasted_iota(jnp.int32, (qb, 2 * HEAD_DIM), 1) >= HEAD_DIM
    for hp in range(ATT_HEADS // 2):
        sl = slice(hp * 2 * HEAD_DIM, (hp + 1) * 2 * HEAD_DIM)
        q2, k2, v2 = q[:, sl], k[:, sl], v[:, sl]
        out = None
        for sub in range(2):
            qm = jnp.where(lane_hi == (sub == 1), q2, jnp.zeros_like(q2))
            s = _nt_dot(qm, k2) + bias_ref[hp * 2 + sub]
            if col_bias is not None:
                s = s + col_bias
            m = jnp.max(s, axis=-1, keepdims=True)
            p = jnp.exp(s - m)
            l = jnp.sum(p, axis=-1, keepdims=True)
            o = jnp.dot(p.astype(BF16), v2, preferred_element_type=F32) / l
            out = o if sub == 0 else jnp.where(lane_hi, o, out)
        o_ref[:, sl] = out.astype(BF16)


def _attn_prompt_kernel(q_ref, k0_ref, k1_ref, k2_ref, v0_ref, v1_ref, v2_ref, bias_ref, o_ref):
    t = pl.program_id(1)
    qb = q_ref.shape[0]
    k = jnp.concatenate([k0_ref[...], k1_ref[...], k2_ref[...]], axis=0).astype(BF16)
    v = jnp.concatenate([v0_ref[...], v1_ref[...], v2_ref[...]], axis=0).astype(BF16)
    key_row = (t - 2) * qb + lax.broadcasted_iota(jnp.int32, (1, 3 * qb), 1)
    col_bias = jnp.where(key_row >= 0, 0.0, NEG_BIG).astype(F32)
    _attend(q_ref[...], k, v, bias_ref, col_bias, o_ref)


def _attn_prompt(q, k, v, bias, n_streams, t_len):
    qb = ATT_QB
    nt = t_len // qb
    cur = lambda b, t: (b * nt + t, 0)
    prev1 = lambda b, t: (b * nt + jnp.maximum(t - 1, 0), 0)
    prev2 = lambda b, t: (b * nt + jnp.maximum(t - 2, 0), 0)
    blk = lambda im: pl.BlockSpec((qb, ATT_W), im)
    return pl.pallas_call(
        _attn_prompt_kernel,
        grid=(n_streams, nt),
        in_specs=[blk(cur), blk(prev2), blk(prev1), blk(cur), blk(prev2), blk(prev1), blk(cur),
                  _const_spec(bias.shape)],
        out_specs=blk(cur),
        out_shape=jax.ShapeDtypeStruct((n_streams * t_len, ATT_W), BF16),
        compiler_params=_cparams(("arbitrary", "arbitrary")),
        name="attn_prompt",
    )(q, k, k, k, v, v, v, bias)


def _attn_sample_kernel(q_ref, kn_ref, vn_ref, kc_ref, vc_ref, bias_ref, o_ref):
    k = jnp.concatenate([kc_ref[0], kn_ref[...]], axis=0).astype(BF16)
    v = jnp.concatenate([vc_ref[0], vn_ref[...]], axis=0).astype(BF16)
    _attend(q_ref[...], k, v, bias_ref, None, o_ref)


def _attn_sample(q, k, v, kc, vc, bias, row0, n_streams, s_len):
    b0 = row0 // s_len
    n_cache = kc.shape[1]
    new = pl.BlockSpec((s_len, ATT_W), lambda s: (b0 + s, 0))
    cache = pl.BlockSpec((1, n_cache, ATT_W), lambda s: (s, 0, 0))
    return pl.pallas_call(
        _attn_sample_kernel,
        grid=(n_streams,),
        in_specs=[new, new, new, cache, cache, _const_spec(bias.shape)],
        out_specs=pl.BlockSpec((s_len, ATT_W), lambda s: (s, 0)),
        out_shape=jax.ShapeDtypeStruct((n_streams * s_len, ATT_W), BF16),
        compiler_params=_cparams(("arbitrary",)),
        name="attn_sample",
    )(q, k, v, kc, vc, bias)


def _merge_kernel(x_ref, ya_ref, yb_ref, yc_ref, gmix_ref, wg_ref, bg_ref, wa_ref, wb_ref, wc_ref,
                  wo_ref, gffn_ref, wq_ref, sk_ref, h_ref, hn_ref, st_ref):
    x = x_ref[...]
    xn = (x * lax.rsqrt(jnp.mean(x * x, axis=-1, keepdims=True) + EPS) * gmix_ref[...]).astype(BF16)
    merged = None
    for b, (y_ref, w_ref) in enumerate(((ya_ref, wa_ref), (yb_ref, wb_ref), (yc_ref, wc_ref))):
        cols = slice(b * D_MODEL, (b + 1) * D_MODEL)
        gate = jax.nn.sigmoid(jnp.dot(xn, wg_ref[:, cols], preferred_element_type=F32) + bg_ref[:, cols])
        term = gate * jnp.dot(y_ref[...], w_ref[...], preferred_element_type=F32)
        merged = term if merged is None else merged + term
    h = x + jnp.dot(merged.astype(BF16), wo_ref[...], preferred_element_type=F32)
    h_ref[...] = h
    hn = (h * lax.rsqrt(jnp.mean(h * h, axis=-1, keepdims=True) + EPS) * gffn_ref[...]).astype(BF16)
    hn_ref[...] = hn
    qp = jnp.dot(hn, wq_ref[...], preferred_element_type=F32).astype(BF16)
    for hp in range(2 * PEER_HEADS):
        st_ref[hp] = _nt_dot(sk_ref[hp % 2], qp[:, hp * PK_HALF:(hp + 1) * PK_HALF])


def _merge(x, ya, yb, yc, gmix, wg, bg, wa, wb, wc, wo, gffn, wq, sk):
    n = x.shape[0]
    tm = ROW_TILE
    row = lambda w: pl.BlockSpec((tm, w), lambda i: (i, 0))
    nq = wq.shape[1]
    return pl.pallas_call(
        _merge_kernel,
        grid=(n // tm,),
        in_specs=[row(D_MODEL), row(POOL_W), row(SGU_W), row(ATT_W),
                  _const_spec((1, D_MODEL)), _const_spec(wg.shape), _const_spec(bg.shape),
                  _const_spec(wa.shape), _const_spec(wb.shape), _const_spec(wc.shape),
                  _const_spec(wo.shape), _const_spec((1, D_MODEL)), _const_spec(wq.shape),
                  _const_spec(sk.shape)],
        out_specs=[row(D_MODEL), row(D_MODEL),
                   pl.BlockSpec((2 * PEER_HEADS, N_KEYS, tm), lambda i: (0, 0, i))],
        out_shape=[jax.ShapeDtypeStruct((n, D_MODEL), F32), jax.ShapeDtypeStruct((n, D_MODEL), BF16),
                   jax.ShapeDtypeStruct((2 * PEER_HEADS, N_KEYS, n), F32)],
        compiler_params=_cparams(("arbitrary",)),
        name="merge",
    )(x, ya, yb, yc, gmix, wg, bg, wa, wb, wc, wo, gffn, wq, sk)


def _top16(x, iota, payloads=()):
    n = x.shape[0]
    vals, idxs = [], []
    picked = [[] for _ in payloads]
    for _ in range(PK_TOPK):
        m = jnp.max(x, axis=0, keepdims=True)
        idx = jnp.min(jnp.where(x == m, iota, float(n)), axis=0, keepdims=True)
        hit = iota == idx
        vals.append(m)
        idxs.append(idx)
        for lst, p in zip(picked, payloads):
            lst.append(jnp.sum(jnp.where(hit, p, 0.0), axis=0, keepdims=True))
        x = jnp.where(hit, -jnp.inf, x)
    return vals, idxs, picked


_PAIRS = [(a, b) for a in range(PK_TOPK) for b in range(PK_TOPK) if (a + 1) * (b + 1) <= PK_TOPK]
_PAIR_ROWS = -(-len(_PAIRS) // SUBLANES) * SUBLANES


def _select_head(s0, s1):
    tt = s0.shape[1]
    iota_k = lax.broadcasted_iota(jnp.int32, (N_KEYS, tt), 0).astype(F32)
    iota_c = lax.broadcasted_iota(jnp.int32, (_PAIR_ROWS, tt), 0).astype(F32)
    pad = _PAIR_ROWS - len(_PAIRS)
    neg_row = jnp.full((1, tt), -jnp.inf, F32)
    zero_row = jnp.zeros((1, tt), F32)
    v0, i0, _ = _top16(s0, iota_k)
    v1, i1, _ = _top16(s1, iota_k)
    cand = jnp.concatenate([v0[a] + v1[b] for a, b in _PAIRS] + [neg_row] * pad, axis=0)
    cand_i = jnp.concatenate([i0[a] for a, _ in _PAIRS] + [zero_row] * pad, axis=0)
    cand_j = jnp.concatenate([i1[b] for _, b in _PAIRS] + [zero_row] * pad, axis=0)
    tv, _, (ti, tj) = _top16(cand, iota_c, (cand_i, cand_j))
    e = jnp.exp(jnp.concatenate(tv, axis=0) - tv[0])
    gates = e / jnp.sum(e, axis=0, keepdims=True)
    return jnp.concatenate(ti, axis=0), jnp.concatenate(tj, axis=0), gates


PEER_STEPS_PER_TILE = 2 * PEER_NC // (PEER_TB // TOPK_TT)
PEER_HEADS_PER_STEP = PEER_HEADS // PEER_STEPS_PER_TILE
PEER_NSEL = PEER_HEADS * PK_TOPK


def _peer_kernel(hn_ref, h_ref, st_ref, uv_ref, o_ref, hs_ref, gm_ref, sel_ref, stage_ref):
    b = pl.program_id(0)
    s = pl.program_id(1)
    tb = hn_ref.shape[0]
    rpc = PEER_SLAB_ROWS
    sps = PEER_SLABS_PER_STEP
    slot_new = b % 2
    slot_cur = 1 - slot_new

    @pl.when((b == 0) & (s == 0))
    def _init():
        sel_ref[...] = jnp.zeros_like(sel_ref)

    def select_step():
        part = s % PEER_STEPS_PER_TILE
        for k in range(PEER_HEADS_PER_STEP):
            rows = _select_head(st_ref[2 * k], st_ref[2 * k + 1])
            r0 = pl.multiple_of(part * (PEER_HEADS_PER_STEP * PK_TOPK), PEER_HEADS_PER_STEP * PK_TOPK)
            for x, val in enumerate(rows):
                stage_ref[x, pl.ds(r0 + k * PK_TOPK, PK_TOPK), :] = val

    @pl.when(s < PEER_NC)
    def _scores():
        hc = _nt_dot(hn_ref[...], uv_ref[...])
        for k in range(sps):
            slab = hs_ref.at[s * sps + k]
            for ib in range(rpc):
                col = (k * rpc + ib) * N_KEYS
                slab[pl.ds(ib, tb, stride=rpc), :] = hc[:, col:col + N_KEYS]
        select_step()

    @pl.when(s == PEER_NC - 1)
    def _weights():
        sub = lax.broadcasted_iota(jnp.int32, (N_KEYS, N_KEYS), 0).astype(F32).astype(BF16)
        one = jnp.ones((N_KEYS, N_KEYS), BF16)
        zero = jnp.zeros((N_KEYS, N_KEYS), BF16)

        def gate_matrix(n):
            irow = sel_ref[slot_cur, 0, pl.ds(n, 1), :].astype(BF16)
            jrow = sel_ref[slot_cur, 1, pl.ds(n, 1), :].astype(BF16)
            g_hi, g_lo = _split_bf16(0.5 * sel_ref[slot_cur, 2, pl.ds(n, 1), :])
            at = jnp.where(sub == irow, one, zero)
            jhit = sub == jrow
            b_hi = jnp.where(jhit, jnp.broadcast_to(g_hi, jhit.shape), zero)
            b_lo = jnp.where(jhit, jnp.broadcast_to(g_lo, jhit.shape), zero)
            return _nt_dot(jnp.concatenate([at, at], axis=1),
                           jnp.concatenate([b_hi, b_lo], axis=1))

        unroll = PEER_TOKEN_UNROLL
        n_groups = tb // unroll

        def make_gates(group, slot):
            group = jnp.minimum(group, n_groups - 1)
            for t in range(unroll):
                gm_ref[slot, t * N_KEYS:(t + 1) * N_KEYS, :] = gate_matrix(group * unroll + t)

        def apply_gates(group, slot):
            base = pl.multiple_of(group * (unroll * rpc), unroll * rpc)
            for c in range(PEER_N_SLABS):
                hv = hs_ref[c, pl.ds(base, unroll * rpc), :]
                gm = jnp.concatenate([gm_ref[slot, t * N_KEYS + c * rpc:t * N_KEYS + (c + 1) * rpc, :]
                                      for t in range(unroll)], axis=0)
                hs_ref[c, pl.ds(base, unroll * rpc), :] = (hv * gm) * (1.0 + lax.erf(hv * (2.0 ** -0.5)))

        def body(m, carry):
            apply_gates(2 * m, 0)
            make_gates(2 * m + 1, 1)
            apply_gates(2 * m + 1, 1)
            make_gates(2 * m + 2, 0)
            return carry

        o_ref[...] = h_ref[...]
        make_gates(0, 0)
        lax.fori_loop(0, n_groups // 2, body, 0)

    @pl.when(s >= PEER_NC)
    def _combine():
        c = s - PEER_NC
        wc = jnp.concatenate([hs_ref.at[c * sps + k][pl.ds(ib, tb, stride=rpc), :]
                              for k in range(sps) for ib in range(rpc)], axis=1)
        o_ref[...] += jnp.dot(wc.astype(BF16), uv_ref[...], preferred_element_type=F32)
        select_step()

    @pl.when(s % PEER_STEPS_PER_TILE == PEER_STEPS_PER_TILE - 1)
    def _publish():
        t0 = pl.multiple_of((s // PEER_STEPS_PER_TILE) * TOPK_TT, TOPK_TT)
        for x in range(3):
            sel_ref[slot_new, x, pl.ds(t0, TOPK_TT), :] = stage_ref[x].T


def _peer(hn, h, st, uv):
    n = hn.shape[0]
    tb = PEER_TB
    nb = n // tb
    nc = PEER_NC
    tiles = tb // TOPK_TT
    row = lambda w: pl.BlockSpec((tb, w), lambda b, s: (jnp.maximum(b - 1, 0), 0))
    st_rows = 2 * PEER_HEADS_PER_STEP
    st_spec = pl.BlockSpec(
        (st_rows, N_KEYS, TOPK_TT),
        lambda b, s: (s % PEER_STEPS_PER_TILE, 0, jnp.minimum(b, nb - 1) * tiles + s // PEER_STEPS_PER_TILE))
    return pl.pallas_call(
        _peer_kernel,
        grid=(nb + 1, 2 * nc),
        in_specs=[row(D_MODEL), row(D_MODEL), st_spec,
                  pl.BlockSpec((None, PEER_EC, D_MODEL), lambda b, s: (s // nc, s % nc, 0))],
        out_specs=row(D_MODEL),
        out_shape=jax.ShapeDtypeStruct((n, D_MODEL), F32),
        scratch_shapes=[pltpu.VMEM((PEER_N_SLABS, tb * PEER_SLAB_ROWS, N_KEYS), F32),
                        pltpu.VMEM((2, PEER_TOKEN_UNROLL * N_KEYS, N_KEYS), F32),
                        pltpu.VMEM((2, 3, tb, PEER_NSEL), F32),
                        pltpu.VMEM((3, PEER_NSEL, TOPK_TT), F32)],
        compiler_params=pltpu.CompilerParams(dimension_semantics=("arbitrary", "arbitrary"),
                                             vmem_limit_bytes=PEER_VMEM_LIMIT),
        name="peer_experts",
    )(hn, h, st, uv)


def _bias_kernel(g_ref, full_ref, band_ref):
    _, nq, nk = full_ref.shape
    w = g_ref.shape[1]
    qc = lax.broadcasted_iota(jnp.int32, (nq, nk), 0) // CHUNK
    kc = lax.broadcasted_iota(jnp.int32, (nq, nk), 1) // CHUNK
    band = jnp.abs(2 * (kc - qc) - BAND // CHUNK) <= BAND // CHUNK
    for h in range(ATT_HEADS):
        line = jnp.broadcast_to(g_ref[h:h + 1, :], (nq, w))
        t = pltpu.roll(line, w - nq + 1, 1, stride=1, stride_axis=0)[:, :nk]
        full_ref[h] = t
        band_ref[h] = jnp.where(band, t, NEG_BIG)


def _rel_bias(table, nq, nk):
    w = 1024
    assert nq + nk - 1 <= w
    dist = nk - 1 - jnp.arange(w)
    line = table[:, jnp.clip(dist, -REL_CLIP, REL_CLIP) + REL_CLIP].astype(F32)
    shape = jax.ShapeDtypeStruct((ATT_HEADS, nq, nk), F32)
    return pl.pallas_call(
        _bias_kernel,
        out_shape=[shape, shape],
        compiler_params=pltpu.CompilerParams(vmem_limit_bytes=VMEM_LIMIT),
        name="rel_bias",
    )(line)


def _block_diag(blocks):
    g, r, c = blocks.shape
    eye = jnp.eye(g, dtype=blocks.dtype)
    return (eye[:, None, :, None] * blocks[:, :, None, :]).reshape(g * r, g * c)


def _sgu_mats(w_s, b_s, length):
    g = w_s.shape[0]
    reps = SGU_CHUNK // length
    tri = jnp.tril(jnp.ones((length, length), bool))
    ws = jnp.where(tri[None], w_s[:, :length, :length], 0.0)
    eye = jnp.eye(reps, dtype=ws.dtype)
    mats = (eye[None, :, None, :, None] * ws[:, None, :, None, :]).reshape(g, SGU_CHUNK, SGU_CHUNK)
    bias = jnp.tile(b_s[:, :length], (1, reps))
    bias = jnp.repeat(bias.T, SGU_GW, axis=1)
    return mats, bias


def kernel(x_prompt, x_sample, state_pool, cache_k, cache_v, g_mix, w_in, pool_w, pool_scale, sgu_norm, sgu_w, sgu_b, q_norm, k_norm, rel_table, w_gate, b_gate, w_br_a, w_br_b, w_br_c, w_out, g_ffn, peer_wq, peer_subkeys, peer_u, peer_v):
    depth = g_mix.shape[0]
    bsz, seq, d = x_prompt.shape
    dbsz, dseq, _ = x_sample.shape
    n_p, n_s = bsz * seq, dbsz * dseq
    n_cache = cache_k.shape[2]
    assert seq % ATT_QB == 0 and n_p % ROW_TILE == 0 and n_s % ROW_TILE == 0
    assert dseq <= SGU_CHUNK and SGU_CHUNK % dseq == 0 and dseq >= POOL_STATE + 1 and dseq % SUBLANES == 0
    assert n_cache == 2 * ATT_QB and dseq <= ATT_QB
    n_pad = -(n_p + n_s) % PEER_TB
    pad_rows = lambda w, dt: [jnp.zeros((n_pad, w), dt)] if n_pad else []
    x = jnp.concatenate([x_prompt.reshape(n_p, d), x_sample.reshape(n_s, d)] + pad_rows(d, F32), axis=0)

    hsum = _block_diag(jnp.full((ATT_HEADS, HEAD_DIM, HEAD_DIM), 1.0 / HEAD_DIM, F32)).astype(BF16)
    zero_prefix = jnp.zeros((bsz, HALO, POOL_W), F32)
    keep = min(BAND, seq)
    outs = {name: [] for name in ("pool_p", "pool_s", "k_p", "v_p", "k_s", "v_s", "sgu_s")}

    for l in range(depth):
        row = lambda a: a[l].reshape(1, -1)
        m_p, b_p = _sgu_mats(sgu_w[l], sgu_b[l], SGU_CHUNK)
        m_s, b_s = _sgu_mats(sgu_w[l], sgu_b[l], dseq)
        ms = jnp.stack([m_p, m_s]).astype(BF16)
        bs = jnp.stack([b_p, b_s])
        a, yb, vn, q, k, v = _in_proj(
            x, row(g_mix), w_in[l].astype(BF16), row(sgu_norm),
            jnp.tile(q_norm[l], ATT_HEADS).reshape(1, -1), jnp.tile(k_norm[l], ATT_HEADS).reshape(1, -1),
            hsum, ms, bs, n_p // ROW_TILE)

        pw = _block_diag(pool_w[l]).astype(BF16)
        sample_prefix = jnp.pad(state_pool[l], ((0, 0), (HALO - POOL_STATE, 0), (0, 0)))
        ya_p = _pool(a, zero_prefix, pw, row(pool_scale), 0, bsz, seq, ROW_TILE)
        ya_s = _pool(a, sample_prefix, pw, row(pool_scale), n_p, dbsz, dseq, dseq)

        bias_full, bias_band = _rel_bias(rel_table[l], ATT_QB, 3 * ATT_QB)
        yc_p = _attn_prompt(q, k, v, bias_band, bsz, seq)
        yc_s = _attn_sample(q, k, v, cache_k[l].reshape(dbsz, n_cache, ATT_W),
                            cache_v[l].reshape(dbsz, n_cache, ATT_W),
                            bias_full[:, :dseq, :n_cache + dseq], n_p, dbsz, dseq)

        h, hn, st = _merge(
            x, jnp.concatenate([ya_p, ya_s] + pad_rows(POOL_W, BF16), axis=0), yb,
            jnp.concatenate([yc_p, yc_s] + pad_rows(ATT_W, BF16), axis=0),
            row(g_mix), w_gate[l].astype(BF16), row(b_gate), w_br_a[l].astype(BF16),
            w_br_b[l].astype(BF16), w_br_c[l].astype(BF16), w_out[l].astype(BF16), row(g_ffn),
            peer_wq[l].astype(BF16), peer_subkeys[l].astype(BF16))
        x = _peer(hn, h, st, jnp.stack([peer_u[l].astype(BF16), peer_v[l].astype(BF16)]))

        def tail(arr, count):
            return jnp.stack([arr[(b + 1) * seq - count:(b + 1) * seq] for b in range(bsz)])

        sample = lambda arr: arr[n_p:n_p + n_s]
        outs["pool_p"].append(tail(a, POOL_STATE))
        outs["pool_s"].append(sample(a).reshape(dbsz, dseq, POOL_W)[:, dseq - POOL_STATE:])
        outs["k_p"].append(tail(k, keep).reshape(bsz, keep, ATT_HEADS, HEAD_DIM))
        outs["v_p"].append(tail(v, keep).reshape(bsz, keep, ATT_HEADS, HEAD_DIM))
        outs["k_s"].append(sample(k).reshape(dbsz, dseq, ATT_HEADS, HEAD_DIM))
        outs["v_s"].append(sample(v).reshape(dbsz, dseq, ATT_HEADS, HEAD_DIM))
        outs["sgu_s"].append(sample(vn).reshape(dbsz, dseq, SGU_W))

    st = lambda name: jnp.stack(outs[name])
    return (x[:n_p].reshape(bsz, seq, d), x[n_p:n_p + n_s].reshape(dbsz, dseq, d), st("pool_p"), st("pool_s"),
            st("k_p"), st("v_p"), st("k_s"), st("v_s"), st("sgu_s"))
```

```python
import functools

import jax
import jax.numpy as jnp
from jax import lax
from jax.experimental import pallas as pl
from jax.experimental.pallas import tpu as pltpu

F32 = jnp.float32
BF16 = jnp.bfloat16

EPS = 1e-6
D_MODEL = 1024
POOL_W = 256
POOL_GW = 64
POOL_STATE = 15
SGU_W = 256
SGU_GW = 64
SGU_CHUNK = 128
ATT_HEADS = 8
HEAD_DIM = 64
ATT_W = ATT_HEADS * HEAD_DIM
CHUNK = 64
BAND = 512
REL_CLIP = 128
N_BRANCH = 3
PEER_HEADS = 8
N_KEYS = 128
PK_HALF = 128
PK_TOPK = 16
NEG_BIG = -1e30

LANES = 128
SUBLANES = 8
ROW_TILE = 256
ATT_QB = 256
HALO = 16
TOPK_TT = 128
PEER_SLAB_ROWS = SUBLANES
PEER_N_SLABS = N_KEYS // PEER_SLAB_ROWS
PEER_SLABS_PER_STEP = 2
PEER_EC = PEER_SLABS_PER_STEP * PEER_SLAB_ROWS * N_KEYS
PEER_NC = N_KEYS * N_KEYS // PEER_EC
PEER_TOKEN_UNROLL = 8
PEER_TB = 512
VMEM_LIMIT = 56 * 1024 * 1024
PEER_VMEM_LIMIT = 60 * 1024 * 1024


def _cparams(sem):
    return pltpu.CompilerParams(dimension_semantics=sem, vmem_limit_bytes=VMEM_LIMIT)


def _const_spec(shape):
    n = len(shape)
    return pl.BlockSpec(shape, lambda *_: (0,) * n)


def _nt_dot(a, b):
    return lax.dot_general(a, b, (((1,), (1,)), ((), ())), preferred_element_type=F32)


def _split_bf16(x):
    hi = x.astype(BF16)
    lo = (x - hi.astype(F32)).astype(BF16)
    return hi, lo


def _in_proj_kernel(x_ref, gmix_ref, win_ref, sgn_ref, qn_ref, kn_ref, hsum_ref, ms_ref, bs_ref,
                    a_ref, yb_ref, vn_ref, q_ref, k_ref, v_ref):
    x = x_ref[...]
    xn = x * lax.rsqrt(jnp.mean(x * x, axis=-1, keepdims=True) + EPS) * gmix_ref[...]
    z = jnp.dot(xn.astype(BF16), win_ref[...], preferred_element_type=F32)
    a_ref[...] = z[:, 0:POOL_W]
    u = z[:, POOL_W:POOL_W + SGU_W]
    vb = z[:, POOL_W + SGU_W:POOL_W + 2 * SGU_W]
    vn = vb * lax.rsqrt(jnp.mean(vb * vb, axis=-1, keepdims=True) + EPS) * sgn_ref[...]
    vn_ref[...] = vn
    vnb = vn.astype(BF16)
    lane_group = lax.broadcasted_iota(jnp.int32, (SGU_CHUNK, SGU_W), 1) // SGU_GW
    rows = x.shape[0]
    for c in range(rows // SGU_CHUNK):
        sl = slice(c * SGU_CHUNK, (c + 1) * SGU_CHUNK)
        vc = vnb[sl, :]
        mixed = jnp.dot(ms_ref[0, 0], vc, preferred_element_type=F32)
        for g in range(1, SGU_W // SGU_GW):
            mg = jnp.dot(ms_ref[0, g], vc, preferred_element_type=F32)
            mixed = jnp.where(lane_group == g, mg, mixed)
        yb_ref[sl, :] = (u[sl, :] * (mixed + bs_ref[0])).astype(BF16)

    base = POOL_W + 2 * SGU_W
    q = z[:, base:base + ATT_W]
    k = z[:, base + ATT_W:base + 2 * ATT_W]
    v_ref[...] = z[:, base + 2 * ATT_W:base + 3 * ATT_W]

    def head_norm(t, w_ref):
        hi, lo = _split_bf16(t * t)
        m = (jnp.dot(hi, hsum_ref[...], preferred_element_type=F32)
             + jnp.dot(lo, hsum_ref[...], preferred_element_type=F32))
        return t * lax.rsqrt(m + EPS) * w_ref[...]

    q_ref[...] = (head_norm(q, qn_ref) * (HEAD_DIM ** -0.5)).astype(BF16)
    k_ref[...] = head_norm(k, kn_ref)


def _in_proj(x, gmix, win, sgn, qn, kn, hsum, ms, bs, n_prompt_tiles):
    n = x.shape[0]
    tm = ROW_TILE
    in_w = win.shape[1]
    row = lambda w: pl.BlockSpec((tm, w), lambda i: (i, 0))
    sel = lambda i: (jnp.where(i >= n_prompt_tiles, 1, 0), 0, 0, 0)
    sel3 = lambda i: (jnp.where(i >= n_prompt_tiles, 1, 0), 0, 0)
    return pl.pallas_call(
        _in_proj_kernel,
        grid=(n // tm,),
        in_specs=[row(D_MODEL), _const_spec((1, D_MODEL)), _const_spec((D_MODEL, in_w)),
                  _const_spec((1, SGU_W)), _const_spec((1, ATT_W)), _const_spec((1, ATT_W)),
                  _const_spec((ATT_W, ATT_W)),
                  pl.BlockSpec((1, SGU_W // SGU_GW, SGU_CHUNK, SGU_CHUNK), sel),
                  pl.BlockSpec((1, SGU_CHUNK, SGU_W), sel3)],
        out_specs=[row(POOL_W), row(SGU_W), row(SGU_W), row(ATT_W), row(ATT_W), row(ATT_W)],
        out_shape=[jax.ShapeDtypeStruct((n, POOL_W), F32), jax.ShapeDtypeStruct((n, SGU_W), BF16),
                   jax.ShapeDtypeStruct((n, SGU_W), F32), jax.ShapeDtypeStruct((n, ATT_W), BF16),
                   jax.ShapeDtypeStruct((n, ATT_W), F32), jax.ShapeDtypeStruct((n, ATT_W), F32)],
        compiler_params=_cparams(("arbitrary",)),
        name="in_proj",
    )(x, gmix, win, sgn, qn, kn, hsum, ms, bs)


def _pool_kernel(a_ref, prev_ref, pre_ref, pw_ref, sc_ref, y_ref):
    t = pl.program_id(1)
    a = a_ref[...]
    tm = a.shape[0]
    halo = jnp.where(t == 0, pre_ref[0], prev_ref[...])
    e = jnp.concatenate([halo, a], axis=0)
    s2 = e[1:] + e[:-1]
    s4 = s2[2:] + s2[:-2]
    s8 = s4[4:] + s4[:-4]
    s16 = s8[8:] + s8[:-8]
    lg = lax.broadcasted_iota(jnp.int32, (tm, POOL_W), 1) // POOL_GW
    mean = jnp.where(lg == 0, s2[15:] * 0.5,
                     jnp.where(lg == 1, s4[13:] * 0.25,
                               jnp.where(lg == 2, s8[9:] * 0.125, s16[1:] * 0.0625)))
    pooled = mean - a
    y = jnp.dot(pooled.astype(BF16), pw_ref[...], preferred_element_type=F32) * sc_ref[...]
    y_ref[...] = y.astype(BF16)


def _pool(a, prefix, pw, sc, row0, n_streams, t_len, tm):
    nt = t_len // tm
    b0 = row0 // tm
    hb = tm // HALO
    return pl.pallas_call(
        _pool_kernel,
        grid=(n_streams, nt),
        in_specs=[pl.BlockSpec((tm, POOL_W), lambda b, t: (b0 + b * nt + t, 0)),
                  pl.BlockSpec((HALO, POOL_W), lambda b, t: (jnp.maximum((b0 + b * nt + t) * hb - 1, 0), 0)),
                  pl.BlockSpec((1, HALO, POOL_W), lambda b, t: (b, 0, 0)),
                  _const_spec((POOL_W, POOL_W)), _const_spec((1, POOL_W))],
        out_specs=pl.BlockSpec((tm, POOL_W), lambda b, t: (b * nt + t, 0)),
        out_shape=jax.ShapeDtypeStruct((n_streams * t_len, POOL_W), BF16),
        compiler_params=_cparams(("arbitrary", "arbitrary")),
        name="pool_mix",
    )(a, a, prefix, pw, sc)


def _attend(q, k, v, bias_ref, col_bias, o_ref):
    qb = q.shape[0]
    lane_hi = lax.broadcasted_iota(jnp.int32, (qb, 2 * HEAD_DIM), 1) >= HEAD_DIM
    for hp in range(ATT_HEADS // 2):
        sl = slice(hp * 2 * HEAD_DIM, (hp + 1) * 2 * HEAD_DIM)
        q2, k2, v2 = q[:, sl], k[:, sl], v[:, sl]
        out = None
        for sub in range(2):
            qm = jnp.where(lane_hi == (sub == 1), q2, jnp.zeros_like(q2))
            s = _nt_dot(qm, k2) + bias_ref[hp * 2 + sub]
            if col_bias is not None:
                s = s + col_bias
            m = jnp.max(s, axis=-1, keepdims=True)
            p = jnp.exp(s - m)
            l = jnp.sum(p, axis=-1, keepdims=True)
            o = jnp.dot(p.astype(BF16), v2, preferred_element_type=F32) / l
            out = o if sub == 0 else jnp.where(lane_hi, o, out)
        o_ref[:, sl] = out.astype(BF16)


def _attn_prompt_kernel(q_ref, k0_ref, k1_ref, k2_ref, v0_ref, v1_ref, v2_ref, bias_ref, o_ref):
    t = pl.program_id(1)
    qb = q_ref.shape[0]
    k = jnp.concatenate([k0_ref[...], k1_ref[...], k2_ref[...]], axis=0).astype(BF16)
    v = jnp.concatenate([v0_ref[...], v1_ref[...], v2_ref[...]], axis=0).astype(BF16)
    key_row = (t - 2) * qb + lax.broadcasted_iota(jnp.int32, (1, 3 * qb), 1)
    col_bias = jnp.where(key_row >= 0, 0.0, NEG_BIG).astype(F32)
    _attend(q_ref[...], k, v, bias_ref, col_bias, o_ref)


def _attn_prompt(q, k, v, bias, n_streams, t_len):
    qb = ATT_QB
    nt = t_len // qb
    cur = lambda b, t: (b * nt + t, 0)
    prev1 = lambda b, t: (b * nt + jnp.maximum(t - 1, 0), 0)
    prev2 = lambda b, t: (b * nt + jnp.maximum(t - 2, 0), 0)
    blk = lambda im: pl.BlockSpec((qb, ATT_W), im)
    return pl.pallas_call(
        _attn_prompt_kernel,
        grid=(n_streams, nt),
        in_specs=[blk(cur), blk(prev2), blk(prev1), blk(cur), blk(prev2), blk(prev1), blk(cur),
                  _const_spec(bias.shape)],
        out_specs=blk(cur),
        out_shape=jax.ShapeDtypeStruct((n_streams * t_len, ATT_W), BF16),
        compiler_params=_cparams(("arbitrary", "arbitrary")),
        name="attn_prompt",
    )(q, k, k, k, v, v, v, bias)


def _attn_sample_kernel(q_ref, kn_ref, vn_ref, kc_ref, vc_ref, bias_ref, o_ref):
    k = jnp.concatenate([kc_ref[0], kn_ref[...]], axis=0).astype(BF16)
    v = jnp.concatenate([vc_ref[0], vn_ref[...]], axis=0).astype(BF16)
    _attend(q_ref[...], k, v, bias_ref, None, o_ref)


def _attn_sample(q, k, v, kc, vc, bias, row0, n_streams, s_len):
    b0 = row0 // s_len
    n_cache = kc.shape[1]
    new = pl.BlockSpec((s_len, ATT_W), lambda s: (b0 + s, 0))
    cache = pl.BlockSpec((1, n_cache, ATT_W), lambda s: (s, 0, 0))
    return pl.pallas_call(
        _attn_sample_kernel,
        grid=(n_streams,),
        in_specs=[new, new, new, cache, cache, _const_spec(bias.shape)],
        out_specs=pl.BlockSpec((s_len, ATT_W), lambda s: (s, 0)),
        out_shape=jax.ShapeDtypeStruct((n_streams * s_len, ATT_W), BF16),
        compiler_params=_cparams(("arbitrary",)),
        name="attn_sample",
    )(q, k, v, kc, vc, bias)


def _merge_kernel(x_ref, yap_ref, yas_ref, yb_ref, ycp_ref, ycs_ref, gmix_ref, wg_ref, bg_ref, wa_ref, wb_ref,
                  wc_ref, wo_ref, gffn_ref, wq_ref, sk_ref, h_ref, hn_ref, st_ref, *, n_prompt_tiles):
    x = x_ref[...]
    xn = (x * lax.rsqrt(jnp.mean(x * x, axis=-1, keepdims=True) + EPS) * gmix_ref[...]).astype(BF16)
    is_prompt = pl.program_id(0) < n_prompt_tiles
    ya = jnp.where(is_prompt, yap_ref[...], yas_ref[...])
    yc = jnp.where(is_prompt, ycp_ref[...], ycs_ref[...])
    merged = None
    for b, (y, w_ref) in enumerate(((ya, wa_ref), (yb_ref[...], wb_ref), (yc, wc_ref))):
        cols = slice(b * D_MODEL, (b + 1) * D_MODEL)
        gate = jax.nn.sigmoid(jnp.dot(xn, wg_ref[:, cols], preferred_element_type=F32) + bg_ref[:, cols])
        term = gate * jnp.dot(y, w_ref[...], preferred_element_type=F32)
        merged = term if merged is None else merged + term
    h = x + jnp.dot(merged.astype(BF16), wo_ref[...], preferred_element_type=F32)
    h_ref[...] = h
    hn = (h * lax.rsqrt(jnp.mean(h * h, axis=-1, keepdims=True) + EPS) * gffn_ref[...]).astype(BF16)
    hn_ref[...] = hn
    qp = jnp.dot(hn, wq_ref[...], preferred_element_type=F32).astype(BF16)
    for hp in range(2 * PEER_HEADS):
        st_ref[hp] = _nt_dot(sk_ref[hp % 2], qp[:, hp * PK_HALF:(hp + 1) * PK_HALF])


def _merge(x, ya_p, ya_s, yb, yc_p, yc_s, gmix, wg, bg, wa, wb, wc, wo, gffn, wq, sk):
    n = x.shape[0]
    tm = ROW_TILE
    npt, nst = ya_p.shape[0] // tm, ya_s.shape[0] // tm
    row = lambda w: pl.BlockSpec((tm, w), lambda i: (i, 0))
    prow = lambda w: pl.BlockSpec((tm, w), lambda i: (jnp.minimum(i, npt - 1), 0))
    srow = lambda w: pl.BlockSpec((tm, w), lambda i: (jnp.clip(i - npt, 0, nst - 1), 0))
    return pl.pallas_call(
        functools.partial(_merge_kernel, n_prompt_tiles=npt),
        grid=(n // tm,),
        in_specs=[row(D_MODEL), prow(POOL_W), srow(POOL_W), row(SGU_W), prow(ATT_W), srow(ATT_W),
                  _const_spec((1, D_MODEL)), _const_spec(wg.shape), _const_spec(bg.shape),
                  _const_spec(wa.shape), _const_spec(wb.shape), _const_spec(wc.shape),
                  _const_spec(wo.shape), _const_spec((1, D_MODEL)), _const_spec(wq.shape),
                  _const_spec(sk.shape)],
        out_specs=[row(D_MODEL), row(D_MODEL),
                   pl.BlockSpec((2 * PEER_HEADS, N_KEYS, tm), lambda i: (0, 0, i))],
        out_shape=[jax.ShapeDtypeStruct((n, D_MODEL), F32), jax.ShapeDtypeStruct((n, D_MODEL), BF16),
                   jax.ShapeDtypeStruct((2 * PEER_HEADS, N_KEYS, n), F32)],
        compiler_params=_cparams(("arbitrary",)),
        name="merge",
    )(x, ya_p, ya_s, yb, yc_p, yc_s, gmix, wg, bg, wa, wb, wc, wo, gffn, wq, sk)


def _top16(x, iota, payloads=()):
    n = x.shape[0]
    vals, idxs = [], []
    picked = [[] for _ in payloads]
    for _ in range(PK_TOPK):
        m = jnp.max(x, axis=0, keepdims=True)
        idx = jnp.min(jnp.where(x == m, iota, float(n)), axis=0, keepdims=True)
        hit = iota == idx
        vals.append(m)
        idxs.append(idx)
        for lst, p in zip(picked, payloads):
            lst.append(jnp.sum(jnp.where(hit, p, 0.0), axis=0, keepdims=True))
        x = jnp.where(hit, -jnp.inf, x)
    return vals, idxs, picked


_KEY_LISTS = 4


def _top16_keys(x):
    n, tt = x.shape
    rows = n // _KEY_LISTS
    base = lax.broadcasted_iota(jnp.int32, (rows, tt), 0).astype(F32)
    val = [x[l * rows:(l + 1) * rows] for l in range(_KEY_LISTS)]
    idx = [base + float(l * rows) for l in range(_KEY_LISTS)]
    for a in [p for end in range(_KEY_LISTS - 1, 0, -1) for p in range(end)]:
        swap = val[a] < val[a + 1]
        val[a], val[a + 1] = jnp.where(swap, val[a + 1], val[a]), jnp.where(swap, val[a], val[a + 1])
        idx[a], idx[a + 1] = jnp.where(swap, idx[a + 1], idx[a]), jnp.where(swap, idx[a], idx[a + 1])
    vals, idxs = [], []
    for _ in range(PK_TOPK):
        m = jnp.max(val[0], axis=0, keepdims=True)
        sel = jnp.min(jnp.where(val[0] == m, idx[0], float(n)), axis=0, keepdims=True)
        hit = idx[0] == sel
        vals.append(m)
        idxs.append(sel)
        for l in range(_KEY_LISTS - 1):
            val[l] = jnp.where(hit, val[l + 1], val[l])
            idx[l] = jnp.where(hit, idx[l + 1], idx[l])
        val[-1] = jnp.where(hit, -jnp.inf, val[-1])
    return vals, idxs


_PAIRS = [(a, b) for a in range(PK_TOPK) for b in range(PK_TOPK) if (a + 1) * (b + 1) <= PK_TOPK]
_PAIR_ROWS = -(-len(_PAIRS) // SUBLANES) * SUBLANES


def _select_head(s0, s1):
    tt = s0.shape[1]
    iota_c = lax.broadcasted_iota(jnp.int32, (_PAIR_ROWS, tt), 0).astype(F32)
    pad = _PAIR_ROWS - len(_PAIRS)
    neg_row = jnp.full((1, tt), -jnp.inf, F32)
    zero_row = jnp.zeros((1, tt), F32)
    v0, i0 = _top16_keys(s0)
    v1, i1 = _top16_keys(s1)
    cand = jnp.concatenate([v0[a] + v1[b] for a, b in _PAIRS] + [neg_row] * pad, axis=0)
    cand_i = jnp.concatenate([i0[a] for a, _ in _PAIRS] + [zero_row] * pad, axis=0)
    cand_j = jnp.concatenate([i1[b] for _, b in _PAIRS] + [zero_row] * pad, axis=0)
    tv, _, (ti, tj) = _top16(cand, iota_c, (cand_i, cand_j))
    e = jnp.exp(jnp.concatenate(tv, axis=0) - tv[0])
    gates = e / jnp.sum(e, axis=0, keepdims=True)
    return jnp.concatenate(ti, axis=0), jnp.concatenate(tj, axis=0), gates


PEER_STEPS_PER_TILE = 2 * PEER_NC // (PEER_TB // TOPK_TT)
PEER_HEADS_PER_STEP = PEER_HEADS // PEER_STEPS_PER_TILE
PEER_NSEL = PEER_HEADS * PK_TOPK


def _peer_kernel(hn_ref, h_ref, st_ref, uv_ref, o_ref, hs_ref, gm_ref, sel_ref, stage_ref):
    b = pl.program_id(0)
    s = pl.program_id(1)
    tb = hn_ref.shape[0]
    rpc = PEER_SLAB_ROWS
    sps = PEER_SLABS_PER_STEP
    slot_new = b % 2
    slot_cur = 1 - slot_new
    first = b == 0
    last = b == pl.num_programs(0) - 1
    middle = jnp.logical_not(jnp.logical_or(first, last))

    def select_step():
        part = s % PEER_STEPS_PER_TILE
        rows = None
        for k in range(PEER_HEADS_PER_STEP):
            s0, s1 = st_ref[2 * k], st_ref[2 * k + 1]
            if rows is not None:
                anchor = 0.0 * rows[2][0:1, :]
                s0, s1 = s0 + anchor, s1 + anchor
            rows = _select_head(s0, s1)
            r0 = pl.multiple_of(part * (PEER_HEADS_PER_STEP * PK_TOPK), PEER_HEADS_PER_STEP * PK_TOPK)
            for x, val in enumerate(rows):
                stage_ref[x, pl.ds(r0 + k * PK_TOPK, PK_TOPK), :] = val

    def scores_pass():
        hc = _nt_dot(hn_ref[...], uv_ref[...])
        for k in range(sps):
            slab = hs_ref.at[s * sps + k]
            for ib in range(rpc):
                col = (k * rpc + ib) * N_KEYS
                slab[pl.ds(ib, tb, stride=rpc), :] = hc[:, col:col + N_KEYS]

    def combine_pass():
        c = s - PEER_NC
        wc = jnp.concatenate([hs_ref.at[c * sps + k][pl.ds(ib, tb, stride=rpc), :]
                              for k in range(sps) for ib in range(rpc)], axis=1)
        o_ref[...] += jnp.dot(wc.astype(BF16), uv_ref[...], preferred_element_type=F32)

    @pl.when(first)
    def _():
        select_step()

    @pl.when(jnp.logical_and(s < PEER_NC, middle))
    def _():
        scores_pass()
        select_step()

    @pl.when(jnp.logical_and(s < PEER_NC, last))
    def _():
        scores_pass()

    @pl.when(jnp.logical_and(s == PEER_NC - 1, jnp.logical_not(first)))
    def _weights():
        sub = lax.broadcasted_iota(jnp.int32, (N_KEYS, N_KEYS), 0).astype(F32).astype(BF16)
        one = jnp.ones((N_KEYS, N_KEYS), BF16)
        zero = jnp.zeros((N_KEYS, N_KEYS), BF16)

        def gate_matrix(n):
            irow = sel_ref[slot_cur, 0, pl.ds(n, 1), :].astype(BF16)
            jrow = sel_ref[slot_cur, 1, pl.ds(n, 1), :].astype(BF16)
            g_hi, g_lo = _split_bf16(0.5 * sel_ref[slot_cur, 2, pl.ds(n, 1), :])
            at = jnp.where(sub == irow, one, zero)
            jhit = sub == jrow
            b_hi = jnp.where(jhit, jnp.broadcast_to(g_hi, jhit.shape), zero)
            b_lo = jnp.where(jhit, jnp.broadcast_to(g_lo, jhit.shape), zero)
            return _nt_dot(jnp.concatenate([at, at], axis=1),
                           jnp.concatenate([b_hi, b_lo], axis=1))

        unroll = PEER_TOKEN_UNROLL
        n_groups = tb // unroll

        def make_gates(group, slot):
            group = jnp.minimum(group, n_groups - 1)
            for t in range(unroll):
                gm_ref[slot, t * N_KEYS:(t + 1) * N_KEYS, :] = gate_matrix(group * unroll + t)

        def apply_gates(group, slot):
            base = pl.multiple_of(group * (unroll * rpc), unroll * rpc)
            for c in range(PEER_N_SLABS):
                hv = hs_ref[c, pl.ds(base, unroll * rpc), :]
                gm = jnp.concatenate([gm_ref[slot, t * N_KEYS + c * rpc:t * N_KEYS + (c + 1) * rpc, :]
                                      for t in range(unroll)], axis=0)
                hs_ref[c, pl.ds(base, unroll * rpc), :] = (hv * gm) * (1.0 + lax.erf(hv * (2.0 ** -0.5)))

        def body(m, carry):
            apply_gates(2 * m, 0)
            make_gates(2 * m + 1, 1)
            apply_gates(2 * m + 1, 1)
            make_gates(2 * m + 2, 0)
            return carry

        o_ref[...] = h_ref[...]
        make_gates(0, 0)
        lax.fori_loop(0, n_groups // 2, body, 0)

    @pl.when(jnp.logical_and(s >= PEER_NC, middle))
    def _():
        combine_pass()
        select_step()

    @pl.when(jnp.logical_and(s >= PEER_NC, last))
    def _():
        combine_pass()

    @pl.when(jnp.logical_and(s % PEER_STEPS_PER_TILE == PEER_STEPS_PER_TILE - 1, jnp.logical_not(last)))
    def _publish():
        t0 = pl.multiple_of((s // PEER_STEPS_PER_TILE) * TOPK_TT, TOPK_TT)
        for x in range(3):
            sel_ref[slot_new, x, pl.ds(t0, TOPK_TT), :] = stage_ref[x].T


def _peer(hn, h, st, uv):
    n = hn.shape[0]
    tb = PEER_TB
    nb = n // tb
    nc = PEER_NC
    tiles = tb // TOPK_TT
    row = lambda w: pl.BlockSpec((tb, w), lambda b, s: (jnp.maximum(b - 1, 0), 0))
    st_rows = 2 * PEER_HEADS_PER_STEP
    st_spec = pl.BlockSpec(
        (st_rows, N_KEYS, TOPK_TT),
        lambda b, s: (s % PEER_STEPS_PER_TILE, 0, jnp.minimum(b, nb - 1) * tiles + s // PEER_STEPS_PER_TILE))
    return pl.pallas_call(
        _peer_kernel,
        grid=(nb + 1, 2 * nc),
        in_specs=[row(D_MODEL), row(D_MODEL), st_spec,
                  pl.BlockSpec((None, PEER_EC, D_MODEL), lambda b, s: (s // nc, s % nc, 0))],
        out_specs=row(D_MODEL),
        out_shape=jax.ShapeDtypeStruct((n, D_MODEL), F32),
        scratch_shapes=[pltpu.VMEM((PEER_N_SLABS, tb * PEER_SLAB_ROWS, N_KEYS), F32),
                        pltpu.VMEM((2, PEER_TOKEN_UNROLL * N_KEYS, N_KEYS), F32),
                        pltpu.VMEM((2, 3, tb, PEER_NSEL), F32),
                        pltpu.VMEM((3, PEER_NSEL, TOPK_TT), F32)],
        compiler_params=pltpu.CompilerParams(dimension_semantics=("arbitrary", "arbitrary"),
                                             vmem_limit_bytes=PEER_VMEM_LIMIT),
        name="peer_experts",
    )(hn, h, st, uv)


def _bias_kernel(g_ref, full_ref, band_ref):
    _, nq, nk = full_ref.shape
    w = g_ref.shape[1]
    qc = lax.broadcasted_iota(jnp.int32, (nq, nk), 0) // CHUNK
    kc = lax.broadcasted_iota(jnp.int32, (nq, nk), 1) // CHUNK
    band = jnp.abs(2 * (kc - qc) - BAND // CHUNK) <= BAND // CHUNK
    for h in range(ATT_HEADS):
        line = jnp.broadcast_to(g_ref[h:h + 1, :], (nq, w))
        t = pltpu.roll(line, w - nq + 1, 1, stride=1, stride_axis=0)[:, :nk]
        full_ref[h] = t
        band_ref[h] = jnp.where(band, t, NEG_BIG)


def _rel_bias(table, nq, nk):
    w = 1024
    assert nq + nk - 1 <= w
    dist = nk - 1 - jnp.arange(w)
    line = table[:, jnp.clip(dist, -REL_CLIP, REL_CLIP) + REL_CLIP].astype(F32)
    shape = jax.ShapeDtypeStruct((ATT_HEADS, nq, nk), F32)
    return pl.pallas_call(
        _bias_kernel,
        out_shape=[shape, shape],
        compiler_params=pltpu.CompilerParams(vmem_limit_bytes=VMEM_LIMIT),
        name="rel_bias",
    )(line)


def _block_diag(blocks):
    g, r, c = blocks.shape
    eye = jnp.eye(g, dtype=blocks.dtype)
    return (eye[:, None, :, None] * blocks[:, :, None, :]).reshape(g * r, g * c)


def _sgu_mats(w_s, b_s, length):
    g = w_s.shape[0]
    reps = SGU_CHUNK // length
    tri = jnp.tril(jnp.ones((length, length), bool))
    ws = jnp.where(tri[None], w_s[:, :length, :length], 0.0)
    eye = jnp.eye(reps, dtype=ws.dtype)
    mats = (eye[None, :, None, :, None] * ws[:, None, :, None, :]).reshape(g, SGU_CHUNK, SGU_CHUNK)
    bias = jnp.tile(b_s[:, :length], (1, reps))
    bias = jnp.repeat(bias.T, SGU_GW, axis=1)
    return mats, bias


def kernel(x_prompt, x_sample, state_pool, cache_k, cache_v, g_mix, w_in, pool_w, pool_scale, sgu_norm, sgu_w, sgu_b, q_norm, k_norm, rel_table, w_gate, b_gate, w_br_a, w_br_b, w_br_c, w_out, g_ffn, peer_wq, peer_subkeys, peer_u, peer_v):
    depth = g_mix.shape[0]
    bsz, seq, d = x_prompt.shape
    dbsz, dseq, _ = x_sample.shape
    n_p, n_s = bsz * seq, dbsz * dseq
    n_cache = cache_k.shape[2]
    assert seq % ATT_QB == 0 and n_p % ROW_TILE == 0 and n_s % ROW_TILE == 0
    assert dseq <= SGU_CHUNK and SGU_CHUNK % dseq == 0 and dseq >= POOL_STATE + 1 and dseq % SUBLANES == 0
    assert n_cache == 2 * ATT_QB and dseq <= ATT_QB
    n_pad = -(n_p + n_s) % PEER_TB
    pad_rows = lambda w, dt: [jnp.zeros((n_pad, w), dt)] if n_pad else []
    x = jnp.concatenate([x_prompt.reshape(n_p, d), x_sample.reshape(n_s, d)] + pad_rows(d, F32), axis=0)

    hsum = _block_diag(jnp.full((ATT_HEADS, HEAD_DIM, HEAD_DIM), 1.0 / HEAD_DIM, F32)).astype(BF16)
    zero_prefix = jnp.zeros((bsz, HALO, POOL_W), F32)
    keep = min(BAND, seq)
    outs = {name: [] for name in ("pool_p", "pool_s", "k_p", "v_p", "k_s", "v_s", "sgu_s")}

    for l in range(depth):
        row = lambda a: a[l].reshape(1, -1)
        m_p, b_p = _sgu_mats(sgu_w[l], sgu_b[l], SGU_CHUNK)
        m_s, b_s = _sgu_mats(sgu_w[l], sgu_b[l], dseq)
        ms = jnp.stack([m_p, m_s]).astype(BF16)
        bs = jnp.stack([b_p, b_s])
        a, yb, vn, q, k, v = _in_proj(
            x, row(g_mix), w_in[l].astype(BF16), row(sgu_norm),
            jnp.tile(q_norm[l], ATT_HEADS).reshape(1, -1), jnp.tile(k_norm[l], ATT_HEADS).reshape(1, -1),
            hsum, ms, bs, n_p // ROW_TILE)

        pw = _block_diag(pool_w[l]).astype(BF16)
        sample_prefix = jnp.pad(state_pool[l], ((0, 0), (HALO - POOL_STATE, 0), (0, 0)))
        ya_p = _pool(a, zero_prefix, pw, row(pool_scale), 0, bsz, seq, ROW_TILE)
        ya_s = _pool(a, sample_prefix, pw, row(pool_scale), n_p, dbsz, dseq, dseq)

        bias_full, bias_band = _rel_bias(rel_table[l], ATT_QB, 3 * ATT_QB)
        yc_p = _attn_prompt(q, k, v, bias_band, bsz, seq)
        yc_s = _attn_sample(q, k, v, cache_k[l].reshape(dbsz, n_cache, ATT_W),
                            cache_v[l].reshape(dbsz, n_cache, ATT_W),
                            bias_full[:, :dseq, :n_cache + dseq], n_p, dbsz, dseq)

        h, hn, st = _merge(
            x, ya_p, ya_s, yb, yc_p, yc_s,
            row(g_mix), w_gate[l].astype(BF16), row(b_gate), w_br_a[l].astype(BF16),
            w_br_b[l].astype(BF16), w_br_c[l].astype(BF16), w_out[l].astype(BF16), row(g_ffn),
            peer_wq[l].astype(BF16), peer_subkeys[l].astype(BF16))
        x = _peer(hn, h, st, jnp.stack([peer_u[l].astype(BF16), peer_v[l].astype(BF16)]))

        def tail(arr, count):
            return jnp.stack([arr[(b + 1) * seq - count:(b + 1) * seq] for b in range(bsz)])

        sample = lambda arr: arr[n_p:n_p + n_s]
        outs["pool_p"].append(tail(a, POOL_STATE))
        outs["pool_s"].append(sample(a).reshape(dbsz, dseq, POOL_W)[:, dseq - POOL_STATE:])
        outs["k_p"].append(tail(k, keep).reshape(bsz, keep, ATT_HEADS, HEAD_DIM))
        outs["v_p"].append(tail(v, keep).reshape(bsz, keep, ATT_HEADS, HEAD_DIM))
        outs["k_s"].append(sample(k).reshape(dbsz, dseq, ATT_HEADS, HEAD_DIM))
        outs["v_s"].append(sample(v).reshape(dbsz, dseq, ATT_HEADS, HEAD_DIM))
        outs["sgu_s"].append(sample(vn).reshape(dbsz, dseq, SGU_W))

    st = lambda name: jnp.stack(outs[name])
    return (x[:n_p].reshape(bsz, seq, d), x[n_p:n_p + n_s].reshape(dbsz, dseq, d), st("pool_p"), st("pool_s"),
            st("k_p"), st("v_p"), st("k_s"), st("v_s"), st("sgu_s"))
```

```python
import functools

import jax
import jax.numpy as jnp
from jax import lax
from jax.experimental import pallas as pl
from jax.experimental.pallas import tpu as pltpu

F32 = jnp.float32
BF16 = jnp.bfloat16

EPS = 1e-6
D_MODEL = 1024
POOL_W = 256
POOL_GW = 64
POOL_STATE = 15
SGU_W = 256
SGU_GW = 64
SGU_CHUNK = 128
ATT_HEADS = 8
HEAD_DIM = 64
ATT_W = ATT_HEADS * HEAD_DIM
CHUNK = 64
BAND = 512
REL_CLIP = 128
N_BRANCH = 3
PEER_HEADS = 8
N_KEYS = 128
PK_HALF = 128
PK_TOPK = 16
NEG_BIG = -1e30

LANES = 128
SUBLANES = 8
ROW_TILE = 256
ATT_QB = 256
HALO = 16
TOPK_TT = 128
PEER_SLAB_ROWS = SUBLANES
PEER_N_SLABS = N_KEYS // PEER_SLAB_ROWS
PEER_SLABS_PER_STEP = 2
PEER_EC = PEER_SLABS_PER_STEP * PEER_SLAB_ROWS * N_KEYS
PEER_NC = N_KEYS * N_KEYS // PEER_EC
PEER_TOKEN_UNROLL = 8
PEER_TB = 512
VMEM_LIMIT = 56 * 1024 * 1024
PEER_VMEM_LIMIT = 60 * 1024 * 1024


def _cparams(sem):
    return pltpu.CompilerParams(dimension_semantics=sem, vmem_limit_bytes=VMEM_LIMIT)


def _const_spec(shape):
    n = len(shape)
    return pl.BlockSpec(shape, lambda *_: (0,) * n)


def _nt_dot(a, b):
    return lax.dot_general(a, b, (((1,), (1,)), ((), ())), preferred_element_type=F32)


def _split_bf16(x):
    hi = x.astype(BF16)
    lo = (x - hi.astype(F32)).astype(BF16)
    return hi, lo


def _row_specs(xa, xb, xb_tile0, n_prompt_tiles, width):
    tm = ROW_TILE
    last_b = xb.shape[0] // tm - 1
    return [pl.BlockSpec((tm, width), lambda i: (jnp.minimum(i, n_prompt_tiles - 1), 0)),
            pl.BlockSpec((tm, width), lambda i: (jnp.clip(xb_tile0 + i - n_prompt_tiles, xb_tile0, last_b), 0))]


def _in_proj_kernel(xa_ref, xb_ref, gmix_ref, win_ref, sgn_ref, qn_ref, kn_ref, hsum_ref, ms_ref, bs_ref,
                    a_ref, yb_ref, vn_ref, q_ref, k_ref, v_ref, *, n_prompt_tiles):
    x = jnp.where(pl.program_id(0) < n_prompt_tiles, xa_ref[...], xb_ref[...])
    xn = x * lax.rsqrt(jnp.mean(x * x, axis=-1, keepdims=True) + EPS) * gmix_ref[...]
    z = jnp.dot(xn.astype(BF16), win_ref[...], preferred_element_type=F32)
    a_ref[...] = z[:, 0:POOL_W]
    u = z[:, POOL_W:POOL_W + SGU_W]
    vb = z[:, POOL_W + SGU_W:POOL_W + 2 * SGU_W]
    vn = vb * lax.rsqrt(jnp.mean(vb * vb, axis=-1, keepdims=True) + EPS) * sgn_ref[...]
    vn_ref[...] = vn
    vnb = vn.astype(BF16)
    lane_group = lax.broadcasted_iota(jnp.int32, (SGU_CHUNK, SGU_W), 1) // SGU_GW
    rows = x.shape[0]
    for c in range(rows // SGU_CHUNK):
        sl = slice(c * SGU_CHUNK, (c + 1) * SGU_CHUNK)
        vc = vnb[sl, :]
        mixed = jnp.dot(ms_ref[0, 0], vc, preferred_element_type=F32)
        for g in range(1, SGU_W // SGU_GW):
            mg = jnp.dot(ms_ref[0, g], vc, preferred_element_type=F32)
            mixed = jnp.where(lane_group == g, mg, mixed)
        yb_ref[sl, :] = (u[sl, :] * (mixed + bs_ref[0])).astype(BF16)

    base = POOL_W + 2 * SGU_W
    q = z[:, base:base + ATT_W]
    k = z[:, base + ATT_W:base + 2 * ATT_W]
    v_ref[...] = z[:, base + 2 * ATT_W:base + 3 * ATT_W]

    def head_norm(t, w_ref):
        hi, lo = _split_bf16(t * t)
        m = (jnp.dot(hi, hsum_ref[...], preferred_element_type=F32)
             + jnp.dot(lo, hsum_ref[...], preferred_element_type=F32))
        return t * lax.rsqrt(m + EPS) * w_ref[...]

    q_ref[...] = (head_norm(q, qn_ref) * (HEAD_DIM ** -0.5)).astype(BF16)
    k_ref[...] = head_norm(k, kn_ref)


def _in_proj(xa, xb, xb_tile0, n, gmix, win, sgn, qn, kn, hsum, ms, bs, n_prompt_tiles):
    tm = ROW_TILE
    in_w = win.shape[1]
    row = lambda w: pl.BlockSpec((tm, w), lambda i: (i, 0))
    sel = lambda i: (jnp.where(i >= n_prompt_tiles, 1, 0), 0, 0, 0)
    sel3 = lambda i: (jnp.where(i >= n_prompt_tiles, 1, 0), 0, 0)
    return pl.pallas_call(
        functools.partial(_in_proj_kernel, n_prompt_tiles=n_prompt_tiles),
        grid=(n // tm,),
        in_specs=_row_specs(xa, xb, xb_tile0, n_prompt_tiles, D_MODEL) + [
                  _const_spec((1, D_MODEL)), _const_spec((D_MODEL, in_w)),
                  _const_spec((1, SGU_W)), _const_spec((1, ATT_W)), _const_spec((1, ATT_W)),
                  _const_spec((ATT_W, ATT_W)),
                  pl.BlockSpec((1, SGU_W // SGU_GW, SGU_CHUNK, SGU_CHUNK), sel),
                  pl.BlockSpec((1, SGU_CHUNK, SGU_W), sel3)],
        out_specs=[row(POOL_W), row(SGU_W), row(SGU_W), row(ATT_W), row(ATT_W), row(ATT_W)],
        out_shape=[jax.ShapeDtypeStruct((n, POOL_W), F32), jax.ShapeDtypeStruct((n, SGU_W), BF16),
                   jax.ShapeDtypeStruct((n, SGU_W), F32), jax.ShapeDtypeStruct((n, ATT_W), BF16),
                   jax.ShapeDtypeStruct((n, ATT_W), F32), jax.ShapeDtypeStruct((n, ATT_W), F32)],
        compiler_params=_cparams(("arbitrary",)),
        name="in_proj",
    )(xa, xb, gmix, win, sgn, qn, kn, hsum, ms, bs)


def _pool_kernel(a_ref, prev_ref, pre_ref, pw_ref, sc_ref, y_ref):
    t = pl.program_id(1)
    a = a_ref[...]
    tm = a.shape[0]
    halo = jnp.where(t == 0, pre_ref[0], prev_ref[...])
    e = jnp.concatenate([halo, a], axis=0)
    s2 = e[1:] + e[:-1]
    s4 = s2[2:] + s2[:-2]
    s8 = s4[4:] + s4[:-4]
    s16 = s8[8:] + s8[:-8]
    lg = lax.broadcasted_iota(jnp.int32, (tm, POOL_W), 1) // POOL_GW
    mean = jnp.where(lg == 0, s2[15:] * 0.5,
                     jnp.where(lg == 1, s4[13:] * 0.25,
                               jnp.where(lg == 2, s8[9:] * 0.125, s16[1:] * 0.0625)))
    pooled = mean - a
    y = jnp.dot(pooled.astype(BF16), pw_ref[...], preferred_element_type=F32) * sc_ref[...]
    y_ref[...] = y.astype(BF16)


def _pool(a, prefix, pw, sc, row0, n_streams, t_len, tm):
    nt = t_len // tm
    b0 = row0 // tm
    hb = tm // HALO
    return pl.pallas_call(
        _pool_kernel,
        grid=(n_streams, nt),
        in_specs=[pl.BlockSpec((tm, POOL_W), lambda b, t: (b0 + b * nt + t, 0)),
                  pl.BlockSpec((HALO, POOL_W), lambda b, t: (jnp.maximum((b0 + b * nt + t) * hb - 1, 0), 0)),
                  pl.BlockSpec((1, HALO, POOL_W), lambda b, t: (b, 0, 0)),
                  _const_spec((POOL_W, POOL_W)), _const_spec((1, POOL_W))],
        out_specs=pl.BlockSpec((tm, POOL_W), lambda b, t: (b * nt + t, 0)),
        out_shape=jax.ShapeDtypeStruct((n_streams * t_len, POOL_W), BF16),
        compiler_params=_cparams(("arbitrary", "arbitrary")),
        name="pool_mix",
    )(a, a, prefix, pw, sc)


def _attend(q, k, v, bias_ref, col_bias, o_ref):
    qb = q.shape[0]
    lane_hi = lax.broadcasted_iota(jnp.int32, (qb, 2 * HEAD_DIM), 1) >= HEAD_DIM
    for hp in range(ATT_HEADS // 2):
        sl = slice(hp * 2 * HEAD_DIM, (hp + 1) * 2 * HEAD_DIM)
        q2, k2, v2 = q[:, sl], k[:, sl], v[:, sl]
        out = None
        for sub in range(2):
            qm = jnp.where(lane_hi == (sub == 1), q2, jnp.zeros_like(q2))
            s = _nt_dot(qm, k2) + bias_ref[hp * 2 + sub]
            if col_bias is not None:
                s = s + col_bias
            m = jnp.max(s, axis=-1, keepdims=True)
            p = jnp.exp(s - m)
            l = jnp.sum(p, axis=-1, keepdims=True)
            o = jnp.dot(p.astype(BF16), v2, preferred_element_type=F32) / l
            out = o if sub == 0 else jnp.where(lane_hi, o, out)
        o_ref[:, sl] = out.astype(BF16)


def _attn_prompt_kernel(q_ref, k0_ref, k1_ref, k2_ref, v0_ref, v1_ref, v2_ref, bias_ref, o_ref):
    t = pl.program_id(1)
    qb = q_ref.shape[0]
    k = jnp.concatenate([k0_ref[...], k1_ref[...], k2_ref[...]], axis=0).astype(BF16)
    v = jnp.concatenate([v0_ref[...], v1_ref[...], v2_ref[...]], axis=0).astype(BF16)
    key_row = (t - 2) * qb + lax.broadcasted_iota(jnp.int32, (1, 3 * qb), 1)
    col_bias = jnp.where(key_row >= 0, 0.0, NEG_BIG).astype(F32)
    _attend(q_ref[...], k, v, bias_ref, col_bias, o_ref)


def _attn_prompt(q, k, v, bias, n_streams, t_len):
    qb = ATT_QB
    nt = t_len // qb
    cur = lambda b, t: (b * nt + t, 0)
    prev1 = lambda b, t: (b * nt + jnp.maximum(t - 1, 0), 0)
    prev2 = lambda b, t: (b * nt + jnp.maximum(t - 2, 0), 0)
    blk = lambda im: pl.BlockSpec((qb, ATT_W), im)
    return pl.pallas_call(
        _attn_prompt_kernel,
        grid=(n_streams, nt),
        in_specs=[blk(cur), blk(prev2), blk(prev1), blk(cur), blk(prev2), blk(prev1), blk(cur),
                  _const_spec(bias.shape)],
        out_specs=blk(cur),
        out_shape=jax.ShapeDtypeStruct((n_streams * t_len, ATT_W), BF16),
        compiler_params=_cparams(("arbitrary", "arbitrary")),
        name="attn_prompt",
    )(q, k, k, k, v, v, v, bias)


def _attn_sample_kernel(q_ref, kn_ref, vn_ref, kc_ref, vc_ref, bias_ref, o_ref):
    k = jnp.concatenate([kc_ref[0], kn_ref[...]], axis=0).astype(BF16)
    v = jnp.concatenate([vc_ref[0], vn_ref[...]], axis=0).astype(BF16)
    _attend(q_ref[...], k, v, bias_ref, None, o_ref)


def _attn_sample(q, k, v, kc, vc, bias, row0, n_streams, s_len):
    b0 = row0 // s_len
    n_cache = kc.shape[1]
    new = pl.BlockSpec((s_len, ATT_W), lambda s: (b0 + s, 0))
    cache = pl.BlockSpec((1, n_cache, ATT_W), lambda s: (s, 0, 0))
    return pl.pallas_call(
        _attn_sample_kernel,
        grid=(n_streams,),
        in_specs=[new, new, new, cache, cache, _const_spec(bias.shape)],
        out_specs=pl.BlockSpec((s_len, ATT_W), lambda s: (s, 0)),
        out_shape=jax.ShapeDtypeStruct((n_streams * s_len, ATT_W), BF16),
        compiler_params=_cparams(("arbitrary",)),
        name="attn_sample",
    )(q, k, v, kc, vc, bias)


def _merge_kernel(xa_ref, xb_ref, yap_ref, yas_ref, yb_ref, ycp_ref, ycs_ref, gmix_ref, wg_ref, bg_ref, wa_ref,
                  wb_ref, wc_ref, wo_ref, gffn_ref, wq_ref, sk_ref, h_ref, hn_ref, st_ref, *, n_prompt_tiles):
    is_prompt = pl.program_id(0) < n_prompt_tiles
    x = jnp.where(is_prompt, xa_ref[...], xb_ref[...])
    xn = (x * lax.rsqrt(jnp.mean(x * x, axis=-1, keepdims=True) + EPS) * gmix_ref[...]).astype(BF16)
    ya = jnp.where(is_prompt, yap_ref[...], yas_ref[...])
    yc = jnp.where(is_prompt, ycp_ref[...], ycs_ref[...])
    merged = None
    for b, (y, w_ref) in enumerate(((ya, wa_ref), (yb_ref[...], wb_ref), (yc, wc_ref))):
        cols = slice(b * D_MODEL, (b + 1) * D_MODEL)
        gate = jax.nn.sigmoid(jnp.dot(xn, wg_ref[:, cols], preferred_element_type=F32) + bg_ref[:, cols])
        term = gate * jnp.dot(y, w_ref[...], preferred_element_type=F32)
        merged = term if merged is None else merged + term
    h = x + jnp.dot(merged.astype(BF16), wo_ref[...], preferred_element_type=F32)
    h_ref[...] = h
    hn = (h * lax.rsqrt(jnp.mean(h * h, axis=-1, keepdims=True) + EPS) * gffn_ref[...]).astype(BF16)
    hn_ref[...] = hn
    qp = jnp.dot(hn, wq_ref[...], preferred_element_type=F32).astype(BF16)
    for hp in range(2 * PEER_HEADS):
        st_ref[hp] = _nt_dot(sk_ref[hp % 2], qp[:, hp * PK_HALF:(hp + 1) * PK_HALF])


def _merge(xa, xb, xb_tile0, n, ya_p, ya_s, yb, yc_p, yc_s, gmix, wg, bg, wa, wb, wc, wo, gffn, wq, sk):
    tm = ROW_TILE
    npt, nst = ya_p.shape[0] // tm, ya_s.shape[0] // tm
    row = lambda w: pl.BlockSpec((tm, w), lambda i: (i, 0))
    prow = lambda w: pl.BlockSpec((tm, w), lambda i: (jnp.minimum(i, npt - 1), 0))
    srow = lambda w: pl.BlockSpec((tm, w), lambda i: (jnp.clip(i - npt, 0, nst - 1), 0))
    return pl.pallas_call(
        functools.partial(_merge_kernel, n_prompt_tiles=npt),
        grid=(n // tm,),
        in_specs=_row_specs(xa, xb, xb_tile0, npt, D_MODEL) + [
                  prow(POOL_W), srow(POOL_W), row(SGU_W), prow(ATT_W), srow(ATT_W),
                  _const_spec((1, D_MODEL)), _const_spec(wg.shape), _const_spec(bg.shape),
                  _const_spec(wa.shape), _const_spec(wb.shape), _const_spec(wc.shape),
                  _const_spec(wo.shape), _const_spec((1, D_MODEL)), _const_spec(wq.shape),
                  _const_spec(sk.shape)],
        out_specs=[row(D_MODEL), row(D_MODEL),
                   pl.BlockSpec((2 * PEER_HEADS, N_KEYS, tm), lambda i: (0, 0, i))],
        out_shape=[jax.ShapeDtypeStruct((n, D_MODEL), F32), jax.ShapeDtypeStruct((n, D_MODEL), BF16),
                   jax.ShapeDtypeStruct((2 * PEER_HEADS, N_KEYS, n), F32)],
        compiler_params=_cparams(("arbitrary",)),
        name="merge",
    )(xa, xb, ya_p, ya_s, yb, yc_p, yc_s, gmix, wg, bg, wa, wb, wc, wo, gffn, wq, sk)


def _top16(x, iota, payloads=()):
    n = x.shape[0]
    vals, idxs = [], []
    picked = [[] for _ in payloads]
    for _ in range(PK_TOPK):
        m = jnp.max(x, axis=0, keepdims=True)
        idx = jnp.min(jnp.where(x == m, iota, float(n)), axis=0, keepdims=True)
        hit = iota == idx
        vals.append(m)
        idxs.append(idx)
        for lst, p in zip(picked, payloads):
            lst.append(jnp.sum(jnp.where(hit, p, 0.0), axis=0, keepdims=True))
        x = jnp.where(hit, -jnp.inf, x)
    return vals, idxs, picked


_KEY_LISTS = 4


def _top16_keys(x):
    n, tt = x.shape
    rows = n // _KEY_LISTS
    base = lax.broadcasted_iota(jnp.int32, (rows, tt), 0).astype(F32)
    val = [x[l * rows:(l + 1) * rows] for l in range(_KEY_LISTS)]
    idx = [base + float(l * rows) for l in range(_KEY_LISTS)]
    for a in [p for end in range(_KEY_LISTS - 1, 0, -1) for p in range(end)]:
        swap = val[a] < val[a + 1]
        val[a], val[a + 1] = jnp.where(swap, val[a + 1], val[a]), jnp.where(swap, val[a], val[a + 1])
        idx[a], idx[a + 1] = jnp.where(swap, idx[a + 1], idx[a]), jnp.where(swap, idx[a], idx[a + 1])
    vals, idxs = [], []
    for _ in range(PK_TOPK):
        m = jnp.max(val[0], axis=0, keepdims=True)
        sel = jnp.min(jnp.where(val[0] == m, idx[0], float(n)), axis=0, keepdims=True)
        hit = idx[0] == sel
        vals.append(m)
        idxs.append(sel)
        for l in range(_KEY_LISTS - 1):
            val[l] = jnp.where(hit, val[l + 1], val[l])
            idx[l] = jnp.where(hit, idx[l + 1], idx[l])
        val[-1] = jnp.where(hit, -jnp.inf, val[-1])
    return vals, idxs


_PAIRS = [(a, b) for a in range(PK_TOPK) for b in range(PK_TOPK) if (a + 1) * (b + 1) <= PK_TOPK]
_PAIR_ROWS = -(-len(_PAIRS) // SUBLANES) * SUBLANES


def _select_head(s0, s1):
    tt = s0.shape[1]
    iota_c = lax.broadcasted_iota(jnp.int32, (_PAIR_ROWS, tt), 0).astype(F32)
    pad = _PAIR_ROWS - len(_PAIRS)
    neg_row = jnp.full((1, tt), -jnp.inf, F32)
    zero_row = jnp.zeros((1, tt), F32)
    v0, i0 = _top16_keys(s0)
    v1, i1 = _top16_keys(s1)
    cand = jnp.concatenate([v0[a] + v1[b] for a, b in _PAIRS] + [neg_row] * pad, axis=0)
    cand_i = jnp.concatenate([i0[a] for a, _ in _PAIRS] + [zero_row] * pad, axis=0)
    cand_j = jnp.concatenate([i1[b] for _, b in _PAIRS] + [zero_row] * pad, axis=0)
    tv, _, (ti, tj) = _top16(cand, iota_c, (cand_i, cand_j))
    e = jnp.exp(jnp.concatenate(tv, axis=0) - tv[0])
    gates = e / jnp.sum(e, axis=0, keepdims=True)
    return jnp.concatenate(ti, axis=0), jnp.concatenate(tj, axis=0), gates


PEER_STEPS_PER_TILE = 2 * PEER_NC // (PEER_TB // TOPK_TT)
PEER_HEADS_PER_STEP = PEER_HEADS // PEER_STEPS_PER_TILE
PEER_NSEL = PEER_HEADS * PK_TOPK


def _peer_kernel(hn_ref, h_ref, st_ref, uv_ref, o_ref, hs_ref, gm_ref, sel_ref, stage_ref):
    b = pl.program_id(0)
    s = pl.program_id(1)
    tb = hn_ref.shape[0]
    rpc = PEER_SLAB_ROWS
    sps = PEER_SLABS_PER_STEP
    slot_new = b % 2
    slot_cur = 1 - slot_new
    first = b == 0
    last = b == pl.num_programs(0) - 1
    middle = jnp.logical_not(jnp.logical_or(first, last))

    def select_step():
        part = s % PEER_STEPS_PER_TILE
        rows = None
        for k in range(PEER_HEADS_PER_STEP):
            s0, s1 = st_ref[2 * k], st_ref[2 * k + 1]
            if rows is not None:
                anchor = 0.0 * rows[2][0:1, :]
                s0, s1 = s0 + anchor, s1 + anchor
            rows = _select_head(s0, s1)
            r0 = pl.multiple_of(part * (PEER_HEADS_PER_STEP * PK_TOPK), PEER_HEADS_PER_STEP * PK_TOPK)
            for x, val in enumerate(rows):
                stage_ref[x, pl.ds(r0 + k * PK_TOPK, PK_TOPK), :] = val

    def scores_pass():
        hc = _nt_dot(hn_ref[...], uv_ref[...])
        for k in range(sps):
            slab = hs_ref.at[s * sps + k]
            for ib in range(rpc):
                col = (k * rpc + ib) * N_KEYS
                slab[pl.ds(ib, tb, stride=rpc), :] = hc[:, col:col + N_KEYS]

    def combine_pass():
        c = s - PEER_NC
        wc = jnp.concatenate([hs_ref.at[c * sps + k][pl.ds(ib, tb, stride=rpc), :]
                              for k in range(sps) for ib in range(rpc)], axis=1)
        o_ref[...] += jnp.dot(wc.astype(BF16), uv_ref[...], preferred_element_type=F32)

    @pl.when(first)
    def _():
        select_step()

    @pl.when(jnp.logical_and(s < PEER_NC, middle))
    def _():
        scores_pass()
        select_step()

    @pl.when(jnp.logical_and(s < PEER_NC, last))
    def _():
        scores_pass()

    @pl.when(jnp.logical_and(s == PEER_NC - 1, jnp.logical_not(first)))
    def _weights():
        sub = lax.broadcasted_iota(jnp.int32, (N_KEYS, N_KEYS), 0).astype(F32).astype(BF16)
        one = jnp.ones((N_KEYS, N_KEYS), BF16)
        zero = jnp.zeros((N_KEYS, N_KEYS), BF16)

        def gate_matrix(n):
            irow = sel_ref[slot_cur, 0, pl.ds(n, 1), :].astype(BF16)
            jrow = sel_ref[slot_cur, 1, pl.ds(n, 1), :].astype(BF16)
            g_hi, g_lo = _split_bf16(0.5 * sel_ref[slot_cur, 2, pl.ds(n, 1), :])
            at = jnp.where(sub == irow, one, zero)
            jhit = sub == jrow
            b_hi = jnp.where(jhit, jnp.broadcast_to(g_hi, jhit.shape), zero)
            b_lo = jnp.where(jhit, jnp.broadcast_to(g_lo, jhit.shape), zero)
            return _nt_dot(jnp.concatenate([at, at], axis=1),
                           jnp.concatenate([b_hi, b_lo], axis=1))

        unroll = PEER_TOKEN_UNROLL
        n_groups = tb // unroll

        def make_gates(group, slot):
            group = jnp.minimum(group, n_groups - 1)
            for t in range(unroll):
                gm_ref[slot, t * N_KEYS:(t + 1) * N_KEYS, :] = gate_matrix(group * unroll + t)

        def apply_gates(group, slot):
            base = pl.multiple_of(group * (unroll * rpc), unroll * rpc)
            for c in range(PEER_N_SLABS):
                hv = hs_ref[c, pl.ds(base, unroll * rpc), :]
                gm = jnp.concatenate([gm_ref[slot, t * N_KEYS + c * rpc:t * N_KEYS + (c + 1) * rpc, :]
                                      for t in range(unroll)], axis=0)
                hs_ref[c, pl.ds(base, unroll * rpc), :] = (hv * gm) * (1.0 + lax.erf(hv * (2.0 ** -0.5)))

        def body(m, carry):
            apply_gates(2 * m, 0)
            make_gates(2 * m + 1, 1)
            apply_gates(2 * m + 1, 1)
            make_gates(2 * m + 2, 0)
            return carry

        o_ref[...] = h_ref[...]
        make_gates(0, 0)
        lax.fori_loop(0, n_groups // 2, body, 0)

    @pl.when(jnp.logical_and(s >= PEER_NC, middle))
    def _():
        combine_pass()
        select_step()

    @pl.when(jnp.logical_and(s >= PEER_NC, last))
    def _():
        combine_pass()

    @pl.when(jnp.logical_and(s % PEER_STEPS_PER_TILE == PEER_STEPS_PER_TILE - 1, jnp.logical_not(last)))
    def _publish():
        t0 = pl.multiple_of((s // PEER_STEPS_PER_TILE) * TOPK_TT, TOPK_TT)
        for x in range(3):
            sel_ref[slot_new, x, pl.ds(t0, TOPK_TT), :] = stage_ref[x].T


def _cast_tables_kernel(u_ref, v_ref, o_ref):
    o_ref[0] = u_ref[...].astype(BF16)
    o_ref[1] = v_ref[...].astype(BF16)


def _cast_tables(u, v):
    depth, e, d = u.shape
    rc = PEER_EC // 2
    src = pl.BlockSpec((None, rc, d), lambda l, r: (l, r, 0))
    return pl.pallas_call(
        _cast_tables_kernel,
        grid=(depth, e // rc),
        in_specs=[src, src],
        out_specs=pl.BlockSpec((None, 2, rc, d), lambda l, r: (l, 0, r, 0)),
        out_shape=jax.ShapeDtypeStruct((depth, 2, e, d), BF16),
        compiler_params=_cparams(("arbitrary", "arbitrary")),
        name="cast_tables",
    )(u, v)


def _peer(hn, h, st, uv, layer):
    n = hn.shape[0]
    tb = PEER_TB
    nb = n // tb
    nc = PEER_NC
    tiles = tb // TOPK_TT
    row = lambda w: pl.BlockSpec((tb, w), lambda b, s: (jnp.maximum(b - 1, 0), 0))
    st_rows = 2 * PEER_HEADS_PER_STEP
    st_spec = pl.BlockSpec(
        (st_rows, N_KEYS, TOPK_TT),
        lambda b, s: (s % PEER_STEPS_PER_TILE, 0, jnp.minimum(b, nb - 1) * tiles + s // PEER_STEPS_PER_TILE))
    return pl.pallas_call(
        _peer_kernel,
        grid=(nb + 1, 2 * nc),
        in_specs=[row(D_MODEL), row(D_MODEL), st_spec,
                  pl.BlockSpec((None, None, PEER_EC, D_MODEL), lambda b, s: (layer, s // nc, s % nc, 0))],
        out_specs=row(D_MODEL),
        out_shape=jax.ShapeDtypeStruct((n, D_MODEL), F32),
        scratch_shapes=[pltpu.VMEM((PEER_N_SLABS, tb * PEER_SLAB_ROWS, N_KEYS), F32),
                        pltpu.VMEM((2, PEER_TOKEN_UNROLL * N_KEYS, N_KEYS), F32),
                        pltpu.VMEM((2, 3, tb, PEER_NSEL), F32),
                        pltpu.VMEM((3, PEER_NSEL, TOPK_TT), F32)],
        compiler_params=pltpu.CompilerParams(dimension_semantics=("arbitrary", "arbitrary"),
                                             vmem_limit_bytes=PEER_VMEM_LIMIT),
        name="peer_experts",
    )(hn, h, st, uv)


def _bias_kernel(g_ref, full_ref, band_ref):
    _, nq, nk = full_ref.shape
    w = g_ref.shape[1]
    qc = lax.broadcasted_iota(jnp.int32, (nq, nk), 0) // CHUNK
    kc = lax.broadcasted_iota(jnp.int32, (nq, nk), 1) // CHUNK
    band = jnp.abs(2 * (kc - qc) - BAND // CHUNK) <= BAND // CHUNK
    for h in range(ATT_HEADS):
        line = jnp.broadcast_to(g_ref[h:h + 1, :], (nq, w))
        t = pltpu.roll(line, w - nq + 1, 1, stride=1, stride_axis=0)[:, :nk]
        full_ref[h] = t
        band_ref[h] = jnp.where(band, t, NEG_BIG)


def _rel_bias(table, nq, nk):
    w = 1024
    assert nq + nk - 1 <= w
    dist = nk - 1 - jnp.arange(w)
    line = table[:, jnp.clip(dist, -REL_CLIP, REL_CLIP) + REL_CLIP].astype(F32)
    shape = jax.ShapeDtypeStruct((ATT_HEADS, nq, nk), F32)
    return pl.pallas_call(
        _bias_kernel,
        out_shape=[shape, shape],
        compiler_params=pltpu.CompilerParams(vmem_limit_bytes=VMEM_LIMIT),
        name="rel_bias",
    )(line)


def _block_diag(blocks):
    g, r, c = blocks.shape
    eye = jnp.eye(g, dtype=blocks.dtype)
    return (eye[:, None, :, None] * blocks[:, :, None, :]).reshape(g * r, g * c)


def _sgu_mats(w_s, b_s, length):
    g = w_s.shape[0]
    reps = SGU_CHUNK // length
    tri = jnp.tril(jnp.ones((length, length), bool))
    ws = jnp.where(tri[None], w_s[:, :length, :length], 0.0)
    eye = jnp.eye(reps, dtype=ws.dtype)
    mats = (eye[None, :, None, :, None] * ws[:, None, :, None, :]).reshape(g, SGU_CHUNK, SGU_CHUNK)
    bias = jnp.tile(b_s[:, :length], (1, reps))
    bias = jnp.repeat(bias.T, SGU_GW, axis=1)
    return mats, bias


def kernel(x_prompt, x_sample, state_pool, cache_k, cache_v, g_mix, w_in, pool_w, pool_scale, sgu_norm, sgu_w, sgu_b, q_norm, k_norm, rel_table, w_gate, b_gate, w_br_a, w_br_b, w_br_c, w_out, g_ffn, peer_wq, peer_subkeys, peer_u, peer_v):
    depth = g_mix.shape[0]
    bsz, seq, d = x_prompt.shape
    dbsz, dseq, _ = x_sample.shape
    n_p, n_s = bsz * seq, dbsz * dseq
    n_cache = cache_k.shape[2]
    assert seq % ATT_QB == 0 and n_p % ROW_TILE == 0 and n_s % ROW_TILE == 0
    assert dseq <= SGU_CHUNK and SGU_CHUNK % dseq == 0 and dseq >= POOL_STATE + 1 and dseq % SUBLANES == 0
    assert n_cache == 2 * ATT_QB and dseq <= ATT_QB
    n_tot = -(-(n_p + n_s) // PEER_TB) * PEER_TB
    n_prompt_tiles = n_p // ROW_TILE
    xa, xb, xb_tile0 = x_prompt.reshape(n_p, d), x_sample.reshape(n_s, d), 0

    hsum = _block_diag(jnp.full((ATT_HEADS, HEAD_DIM, HEAD_DIM), 1.0 / HEAD_DIM, F32)).astype(BF16)
    zero_prefix = jnp.zeros((bsz, HALO, POOL_W), F32)
    uv_tables = _cast_tables(peer_u, peer_v)
    keep = min(BAND, seq)
    outs = {name: [] for name in ("pool_p", "pool_s", "k_p", "v_p", "k_s", "v_s", "sgu_s")}

    for l in range(depth):
        row = lambda a: a[l].reshape(1, -1)
        m_p, b_p = _sgu_mats(sgu_w[l], sgu_b[l], SGU_CHUNK)
        m_s, b_s = _sgu_mats(sgu_w[l], sgu_b[l], dseq)
        ms = jnp.stack([m_p, m_s]).astype(BF16)
        bs = jnp.stack([b_p, b_s])
        a, yb, vn, q, k, v = _in_proj(
            xa, xb, xb_tile0, n_tot, row(g_mix), w_in[l].astype(BF16), row(sgu_norm),
            jnp.tile(q_norm[l], ATT_HEADS).reshape(1, -1), jnp.tile(k_norm[l], ATT_HEADS).reshape(1, -1),
            hsum, ms, bs, n_prompt_tiles)

        pw = _block_diag(pool_w[l]).astype(BF16)
        sample_prefix = jnp.pad(state_pool[l], ((0, 0), (HALO - POOL_STATE, 0), (0, 0)))
        ya_p = _pool(a, zero_prefix, pw, row(pool_scale), 0, bsz, seq, ROW_TILE)
        ya_s = _pool(a, sample_prefix, pw, row(pool_scale), n_p, dbsz, dseq, dseq)

        bias_full, bias_band = _rel_bias(rel_table[l], ATT_QB, 3 * ATT_QB)
        yc_p = _attn_prompt(q, k, v, bias_band, bsz, seq)
        yc_s = _attn_sample(q, k, v, cache_k[l].reshape(dbsz, n_cache, ATT_W),
                            cache_v[l].reshape(dbsz, n_cache, ATT_W),
                            bias_full[:, :dseq, :n_cache + dseq], n_p, dbsz, dseq)

        h, hn, st = _merge(
            xa, xb, xb_tile0, n_tot, ya_p, ya_s, yb, yc_p, yc_s,
            row(g_mix), w_gate[l].astype(BF16), row(b_gate), w_br_a[l].astype(BF16),
            w_br_b[l].astype(BF16), w_br_c[l].astype(BF16), w_out[l].astype(BF16), row(g_ffn),
            peer_wq[l].astype(BF16), peer_subkeys[l].astype(BF16))
        x = _peer(hn, h, st, uv_tables, l)
        xa, xb, xb_tile0 = x, x, n_prompt_tiles

        def tail(arr, count):
            return jnp.stack([arr[(b + 1) * seq - count:(b + 1) * seq] for b in range(bsz)])

        sample = lambda arr: arr[n_p:n_p + n_s]
        outs["pool_p"].append(tail(a, POOL_STATE))
        outs["pool_s"].append(sample(a).reshape(dbsz, dseq, POOL_W)[:, dseq - POOL_STATE:])
        outs["k_p"].append(tail(k, keep).reshape(bsz, keep, ATT_HEADS, HEAD_DIM))
        outs["v_p"].append(tail(v, keep).reshape(bsz, keep, ATT_HEADS, HEAD_DIM))
        outs["k_s"].append(sample(k).reshape(dbsz, dseq, ATT_HEADS, HEAD_DIM))
        outs["v_s"].append(sample(v).reshape(dbsz, dseq, ATT_HEADS, HEAD_DIM))
        outs["sgu_s"].append(sample(vn).reshape(dbsz, dseq, SGU_W))

    st = lambda name: jnp.stack(outs[name])
    return (x[:n_p].reshape(bsz, seq, d), x[n_p:n_p + n_s].reshape(dbsz, dseq, d), st("pool_p"), st("pool_s"),
            st("k_p"), st("v_p"), st("k_s"), st("v_s"), st("sgu_s"))
```

```python
import functools

import jax
import jax.numpy as jnp
from jax import lax
from jax.experimental import pallas as pl
from jax.experimental.pallas import tpu as pltpu

F32 = jnp.float32
BF16 = jnp.bfloat16

EPS = 1e-6
D_MODEL = 1024
POOL_W = 256
POOL_GW = 64
POOL_STATE = 15
SGU_W = 256
SGU_GW = 64
SGU_CHUNK = 128
ATT_HEADS = 8
HEAD_DIM = 64
ATT_W = ATT_HEADS * HEAD_DIM
CHUNK = 64
BAND = 512
REL_CLIP = 128
N_BRANCH = 3
PEER_HEADS = 8
N_KEYS = 128
PK_HALF = 128
PK_TOPK = 16
NEG_BIG = -1e30

LANES = 128
SUBLANES = 8
ROW_TILE = 256
ATT_QB = 256
HALO = 16
TOPK_TT = 128
PEER_SLAB_ROWS = SUBLANES
PEER_N_SLABS = N_KEYS // PEER_SLAB_ROWS
PEER_SLABS_PER_STEP = 2
PEER_EC = PEER_SLABS_PER_STEP * PEER_SLAB_ROWS * N_KEYS
PEER_NC = N_KEYS * N_KEYS // PEER_EC
PEER_TOKEN_UNROLL = 16
PEER_TB = 512
VMEM_LIMIT = 56 * 1024 * 1024
PEER_VMEM_LIMIT = 60 * 1024 * 1024


def _cparams(sem):
    return pltpu.CompilerParams(dimension_semantics=sem, vmem_limit_bytes=VMEM_LIMIT)


def _const_spec(shape):
    n = len(shape)
    return pl.BlockSpec(shape, lambda *_: (0,) * n)


def _nt_dot(a, b):
    return lax.dot_general(a, b, (((1,), (1,)), ((), ())), preferred_element_type=F32)


def _split_bf16(x):
    hi = x.astype(BF16)
    lo = (x - hi.astype(F32)).astype(BF16)
    return hi, lo


def _row_specs(xa, xb, xb_tile0, n_prompt_tiles, width):
    tm = ROW_TILE
    last_b = xb.shape[0] // tm - 1
    return [pl.BlockSpec((tm, width), lambda i: (jnp.minimum(i, n_prompt_tiles - 1), 0)),
            pl.BlockSpec((tm, width), lambda i: (jnp.clip(xb_tile0 + i - n_prompt_tiles, xb_tile0, last_b), 0))]


def _in_proj_kernel(xa_ref, xb_ref, gmix_ref, win_ref, sgn_ref, qn_ref, kn_ref, hsum_ref, ms_ref, bs_ref,
                    a_ref, yb_ref, vn_ref, q_ref, k_ref, v_ref, *, n_prompt_tiles):
    x = jnp.where(pl.program_id(0) < n_prompt_tiles, xa_ref[...], xb_ref[...])
    xn = x * lax.rsqrt(jnp.mean(x * x, axis=-1, keepdims=True) + EPS) * gmix_ref[...]
    z = jnp.dot(xn.astype(BF16), win_ref[...], preferred_element_type=F32)
    a_ref[...] = z[:, 0:POOL_W]
    u = z[:, POOL_W:POOL_W + SGU_W]
    vb = z[:, POOL_W + SGU_W:POOL_W + 2 * SGU_W]
    vn = vb * lax.rsqrt(jnp.mean(vb * vb, axis=-1, keepdims=True) + EPS) * sgn_ref[...]
    vn_ref[...] = vn
    vnb = vn.astype(BF16)
    lane_group = lax.broadcasted_iota(jnp.int32, (SGU_CHUNK, SGU_W), 1) // SGU_GW
    rows = x.shape[0]
    for c in range(rows // SGU_CHUNK):
        sl = slice(c * SGU_CHUNK, (c + 1) * SGU_CHUNK)
        vc = vnb[sl, :]
        mixed = jnp.dot(ms_ref[0, 0], vc, preferred_element_type=F32)
        for g in range(1, SGU_W // SGU_GW):
            mg = jnp.dot(ms_ref[0, g], vc, preferred_element_type=F32)
            mixed = jnp.where(lane_group == g, mg, mixed)
        yb_ref[sl, :] = (u[sl, :] * (mixed + bs_ref[0])).astype(BF16)

    base = POOL_W + 2 * SGU_W
    q = z[:, base:base + ATT_W]
    k = z[:, base + ATT_W:base + 2 * ATT_W]
    v_ref[...] = z[:, base + 2 * ATT_W:base + 3 * ATT_W]

    def head_norm(t, w_ref):
        m = jnp.dot((t * t).astype(BF16), hsum_ref[...], preferred_element_type=F32)
        return t * lax.rsqrt(m + EPS) * w_ref[...]

    q_ref[...] = (head_norm(q, qn_ref) * (HEAD_DIM ** -0.5)).astype(BF16)
    k_ref[...] = head_norm(k, kn_ref)


def _in_proj(xa, xb, xb_tile0, n, gmix, win, sgn, qn, kn, hsum, ms, bs, n_prompt_tiles):
    tm = ROW_TILE
    in_w = win.shape[1]
    row = lambda w: pl.BlockSpec((tm, w), lambda i: (i, 0))
    sel = lambda i: (jnp.where(i >= n_prompt_tiles, 1, 0), 0, 0, 0)
    sel3 = lambda i: (jnp.where(i >= n_prompt_tiles, 1, 0), 0, 0)
    return pl.pallas_call(
        functools.partial(_in_proj_kernel, n_prompt_tiles=n_prompt_tiles),
        grid=(n // tm,),
        in_specs=_row_specs(xa, xb, xb_tile0, n_prompt_tiles, D_MODEL) + [
                  _const_spec((1, D_MODEL)), _const_spec((D_MODEL, in_w)),
                  _const_spec((1, SGU_W)), _const_spec((1, ATT_W)), _const_spec((1, ATT_W)),
                  _const_spec((ATT_W, ATT_W)),
                  pl.BlockSpec((1, SGU_W // SGU_GW, SGU_CHUNK, SGU_CHUNK), sel),
                  pl.BlockSpec((1, SGU_CHUNK, SGU_W), sel3)],
        out_specs=[row(POOL_W), row(SGU_W), row(SGU_W), row(ATT_W), row(ATT_W), row(ATT_W)],
        out_shape=[jax.ShapeDtypeStruct((n, POOL_W), F32), jax.ShapeDtypeStruct((n, SGU_W), BF16),
                   jax.ShapeDtypeStruct((n, SGU_W), F32), jax.ShapeDtypeStruct((n, ATT_W), BF16),
                   jax.ShapeDtypeStruct((n, ATT_W), F32), jax.ShapeDtypeStruct((n, ATT_W), F32)],
        compiler_params=_cparams(("arbitrary",)),
        name="in_proj",
    )(xa, xb, gmix, win, sgn, qn, kn, hsum, ms, bs)


def _pool_kernel(a_ref, prev_ref, pre_ref, pw_ref, sc_ref, y_ref):
    t = pl.program_id(1)
    a = a_ref[...]
    tm = a.shape[0]
    halo = jnp.where(t == 0, pre_ref[0], prev_ref[...])
    e = jnp.concatenate([halo, a], axis=0)
    s2 = e[1:] + e[:-1]
    s4 = s2[2:] + s2[:-2]
    s8 = s4[4:] + s4[:-4]
    s16 = s8[8:] + s8[:-8]
    lg = lax.broadcasted_iota(jnp.int32, (tm, POOL_W), 1) // POOL_GW
    mean = jnp.where(lg == 0, s2[15:] * 0.5,
                     jnp.where(lg == 1, s4[13:] * 0.25,
                               jnp.where(lg == 2, s8[9:] * 0.125, s16[1:] * 0.0625)))
    pooled = mean - a
    y = jnp.dot(pooled.astype(BF16), pw_ref[...], preferred_element_type=F32) * sc_ref[...]
    y_ref[...] = y.astype(BF16)


def _pool(a, prefix, pw, sc, row0, n_streams, t_len, tm):
    nt = t_len // tm
    b0 = row0 // tm
    hb = tm // HALO
    return pl.pallas_call(
        _pool_kernel,
        grid=(n_streams, nt),
        in_specs=[pl.BlockSpec((tm, POOL_W), lambda b, t: (b0 + b * nt + t, 0)),
                  pl.BlockSpec((HALO, POOL_W), lambda b, t: (jnp.maximum((b0 + b * nt + t) * hb - 1, 0), 0)),
                  pl.BlockSpec((1, HALO, POOL_W), lambda b, t: (b, 0, 0)),
                  _const_spec((POOL_W, POOL_W)), _const_spec((1, POOL_W))],
        out_specs=pl.BlockSpec((tm, POOL_W), lambda b, t: (b * nt + t, 0)),
        out_shape=jax.ShapeDtypeStruct((n_streams * t_len, POOL_W), BF16),
        compiler_params=_cparams(("arbitrary", "arbitrary")),
        name="pool_mix",
    )(a, a, prefix, pw, sc)


def _attend(q, k, v, bias_ref, col_bias, o_ref):
    qb = q.shape[0]
    lane_hi = lax.broadcasted_iota(jnp.int32, (qb, 2 * HEAD_DIM), 1) >= HEAD_DIM
    for hp in range(ATT_HEADS // 2):
        sl = slice(hp * 2 * HEAD_DIM, (hp + 1) * 2 * HEAD_DIM)
        q2, k2, v2 = q[:, sl], k[:, sl], v[:, sl]
        out = None
        for sub in range(2):
            qm = jnp.where(lane_hi == (sub == 1), q2, jnp.zeros_like(q2))
            s = _nt_dot(qm, k2) + bias_ref[hp * 2 + sub]
            if col_bias is not None:
                s = s + col_bias
            m = jnp.max(s, axis=-1, keepdims=True)
            p = jnp.exp(s - m)
            l = jnp.sum(p, axis=-1, keepdims=True)
            o = jnp.dot(p.astype(BF16), v2, preferred_element_type=F32) / l
            out = o if sub == 0 else jnp.where(lane_hi, o, out)
        o_ref[:, sl] = out.astype(BF16)


def _attn_prompt_kernel(q_ref, k0_ref, k1_ref, k2_ref, v0_ref, v1_ref, v2_ref, bias_ref, o_ref):
    t = pl.program_id(1)
    qb = q_ref.shape[0]
    k = jnp.concatenate([k0_ref[...], k1_ref[...], k2_ref[...]], axis=0).astype(BF16)
    v = jnp.concatenate([v0_ref[...], v1_ref[...], v2_ref[...]], axis=0).astype(BF16)
    key_row = (t - 2) * qb + lax.broadcasted_iota(jnp.int32, (1, 3 * qb), 1)
    col_bias = jnp.where(key_row >= 0, 0.0, NEG_BIG).astype(F32)
    _attend(q_ref[...], k, v, bias_ref, col_bias, o_ref)


def _attn_prompt(q, k, v, bias, n_streams, t_len):
    qb = ATT_QB
    nt = t_len // qb
    cur = lambda b, t: (b * nt + t, 0)
    prev1 = lambda b, t: (b * nt + jnp.maximum(t - 1, 0), 0)
    prev2 = lambda b, t: (b * nt + jnp.maximum(t - 2, 0), 0)
    blk = lambda im: pl.BlockSpec((qb, ATT_W), im)
    return pl.pallas_call(
        _attn_prompt_kernel,
        grid=(n_streams, nt),
        in_specs=[blk(cur), blk(prev2), blk(prev1), blk(cur), blk(prev2), blk(prev1), blk(cur),
                  _const_spec(bias.shape)],
        out_specs=blk(cur),
        out_shape=jax.ShapeDtypeStruct((n_streams * t_len, ATT_W), BF16),
        compiler_params=_cparams(("arbitrary", "arbitrary")),
        name="attn_prompt",
    )(q, k, k, k, v, v, v, bias)


def _attn_sample_kernel(q_ref, kn_ref, vn_ref, kc_ref, vc_ref, bias_ref, o_ref):
    k = jnp.concatenate([kc_ref[0], kn_ref[...]], axis=0).astype(BF16)
    v = jnp.concatenate([vc_ref[0], vn_ref[...]], axis=0).astype(BF16)
    _attend(q_ref[...], k, v, bias_ref, None, o_ref)


def _attn_sample(q, k, v, kc, vc, bias, row0, n_streams, s_len):
    b0 = row0 // s_len
    n_cache = kc.shape[1]
    new = pl.BlockSpec((s_len, ATT_W), lambda s: (b0 + s, 0))
    cache = pl.BlockSpec((1, n_cache, ATT_W), lambda s: (s, 0, 0))
    return pl.pallas_call(
        _attn_sample_kernel,
        grid=(n_streams,),
        in_specs=[new, new, new, cache, cache, _const_spec(bias.shape)],
        out_specs=pl.BlockSpec((s_len, ATT_W), lambda s: (s, 0)),
        out_shape=jax.ShapeDtypeStruct((n_streams * s_len, ATT_W), BF16),
        compiler_params=_cparams(("arbitrary",)),
        name="attn_sample",
    )(q, k, v, kc, vc, bias)


def _merge_kernel(xa_ref, xb_ref, yap_ref, yas_ref, yb_ref, ycp_ref, ycs_ref, gmix_ref, wg_ref, bg_ref, wa_ref,
                  wb_ref, wc_ref, wo_ref, gffn_ref, wq_ref, sk_ref, h_ref, hn_ref, st_ref, *, n_prompt_tiles):
    is_prompt = pl.program_id(0) < n_prompt_tiles
    x = jnp.where(is_prompt, xa_ref[...], xb_ref[...])
    xn = (x * lax.rsqrt(jnp.mean(x * x, axis=-1, keepdims=True) + EPS) * gmix_ref[...]).astype(BF16)
    ya = jnp.where(is_prompt, yap_ref[...], yas_ref[...])
    yc = jnp.where(is_prompt, ycp_ref[...], ycs_ref[...])
    merged = None
    for b, (y, w_ref) in enumerate(((ya, wa_ref), (yb_ref[...], wb_ref), (yc, wc_ref))):
        cols = slice(b * D_MODEL, (b + 1) * D_MODEL)
        gate = jax.nn.sigmoid(jnp.dot(xn, wg_ref[:, cols], preferred_element_type=F32) + bg_ref[:, cols])
        term = gate * jnp.dot(y, w_ref[...], preferred_element_type=F32)
        merged = term if merged is None else merged + term
    h = x + jnp.dot(merged.astype(BF16), wo_ref[...], preferred_element_type=F32)
    h_ref[...] = h
    hn = (h * lax.rsqrt(jnp.mean(h * h, axis=-1, keepdims=True) + EPS) * gffn_ref[...]).astype(BF16)
    hn_ref[...] = hn
    qp = jnp.dot(hn, wq_ref[...], preferred_element_type=F32).astype(BF16)
    for hp in range(2 * PEER_HEADS):
        st_ref[hp] = _nt_dot(sk_ref[hp % 2], qp[:, hp * PK_HALF:(hp + 1) * PK_HALF])


def _merge(xa, xb, xb_tile0, n, ya_p, ya_s, yb, yc_p, yc_s, gmix, wg, bg, wa, wb, wc, wo, gffn, wq, sk):
    tm = ROW_TILE
    npt, nst = ya_p.shape[0] // tm, ya_s.shape[0] // tm
    row = lambda w: pl.BlockSpec((tm, w), lambda i: (i, 0))
    prow = lambda w: pl.BlockSpec((tm, w), lambda i: (jnp.minimum(i, npt - 1), 0))
    srow = lambda w: pl.BlockSpec((tm, w), lambda i: (jnp.clip(i - npt, 0, nst - 1), 0))
    return pl.pallas_call(
        functools.partial(_merge_kernel, n_prompt_tiles=npt),
        grid=(n // tm,),
        in_specs=_row_specs(xa, xb, xb_tile0, npt, D_MODEL) + [
                  prow(POOL_W), srow(POOL_W), row(SGU_W), prow(ATT_W), srow(ATT_W),
                  _const_spec((1, D_MODEL)), _const_spec(wg.shape), _const_spec(bg.shape),
                  _const_spec(wa.shape), _const_spec(wb.shape), _const_spec(wc.shape),
                  _const_spec(wo.shape), _const_spec((1, D_MODEL)), _const_spec(wq.shape),
                  _const_spec(sk.shape)],
        out_specs=[row(D_MODEL), row(D_MODEL),
                   pl.BlockSpec((2 * PEER_HEADS, N_KEYS, tm), lambda i: (0, 0, i))],
        out_shape=[jax.ShapeDtypeStruct((n, D_MODEL), F32), jax.ShapeDtypeStruct((n, D_MODEL), BF16),
                   jax.ShapeDtypeStruct((2 * PEER_HEADS, N_KEYS, n), F32)],
        compiler_params=_cparams(("arbitrary",)),
        name="merge",
    )(xa, xb, ya_p, ya_s, yb, yc_p, yc_s, gmix, wg, bg, wa, wb, wc, wo, gffn, wq, sk)


def _top16(x, iota, payloads=()):
    n = x.shape[0]
    vals, idxs = [], []
    picked = [[] for _ in payloads]
    for _ in range(PK_TOPK):
        m = jnp.max(x, axis=0, keepdims=True)
        idx = jnp.min(jnp.where(x == m, iota, float(n)), axis=0, keepdims=True)
        hit = iota == idx
        vals.append(m)
        idxs.append(idx)
        for lst, p in zip(picked, payloads):
            lst.append(jnp.sum(jnp.where(hit, p, 0.0), axis=0, keepdims=True))
        x = jnp.where(hit, -jnp.inf, x)
    return vals, idxs, picked


_KEY_LISTS = 4


def _top16_keys(x):
    n, tt = x.shape
    rows = n // _KEY_LISTS
    base = lax.broadcasted_iota(jnp.int32, (rows, tt), 0).astype(F32)
    val = [x[l * rows:(l + 1) * rows] for l in range(_KEY_LISTS)]
    idx = [base + float(l * rows) for l in range(_KEY_LISTS)]
    for a in [p for end in range(_KEY_LISTS - 1, 0, -1) for p in range(end)]:
        swap = val[a] < val[a + 1]
        val[a], val[a + 1] = jnp.where(swap, val[a + 1], val[a]), jnp.where(swap, val[a], val[a + 1])
        idx[a], idx[a + 1] = jnp.where(swap, idx[a + 1], idx[a]), jnp.where(swap, idx[a], idx[a + 1])
    vals, idxs = [], []
    for _ in range(PK_TOPK):
        m = jnp.max(val[0], axis=0, keepdims=True)
        sel = jnp.min(jnp.where(val[0] == m, idx[0], float(n)), axis=0, keepdims=True)
        hit = idx[0] == sel
        vals.append(m)
        idxs.append(sel)
        for l in range(_KEY_LISTS - 1):
            val[l] = jnp.where(hit, val[l + 1], val[l])
            idx[l] = jnp.where(hit, idx[l + 1], idx[l])
        val[-1] = jnp.where(hit, -jnp.inf, val[-1])
    return vals, idxs


_PAIRS = [(a, b) for a in range(PK_TOPK) for b in range(PK_TOPK) if (a + 1) * (b + 1) <= PK_TOPK]
_PAIR_ROWS = -(-len(_PAIRS) // SUBLANES) * SUBLANES


def _select_head(s0, s1):
    tt = s0.shape[1]
    iota_c = lax.broadcasted_iota(jnp.int32, (_PAIR_ROWS, tt), 0).astype(F32)
    pad = _PAIR_ROWS - len(_PAIRS)
    neg_row = jnp.full((1, tt), -jnp.inf, F32)
    zero_row = jnp.zeros((1, tt), F32)
    v0, i0 = _top16_keys(s0)
    v1, i1 = _top16_keys(s1)
    cand = jnp.concatenate([v0[a] + v1[b] for a, b in _PAIRS] + [neg_row] * pad, axis=0)
    cand_i = jnp.concatenate([i0[a] for a, _ in _PAIRS] + [zero_row] * pad, axis=0)
    cand_j = jnp.concatenate([i1[b] for _, b in _PAIRS] + [zero_row] * pad, axis=0)
    tv, _, (ti, tj) = _top16(cand, iota_c, (cand_i, cand_j))
    e = jnp.exp(jnp.concatenate(tv, axis=0) - tv[0])
    gates = e / jnp.sum(e, axis=0, keepdims=True)
    return jnp.concatenate(ti, axis=0), jnp.concatenate(tj, axis=0), gates


PEER_STEPS_PER_TILE = 2 * PEER_NC // (PEER_TB // TOPK_TT)
PEER_HEADS_PER_STEP = PEER_HEADS // PEER_STEPS_PER_TILE
PEER_NSEL = PEER_HEADS * PK_TOPK


def _peer_kernel(hn_ref, h_ref, st_ref, uv_ref, o_ref, hs_ref, gm_ref, sel_ref, stage_ref):
    b = pl.program_id(0)
    s = pl.program_id(1)
    tb = hn_ref.shape[0]
    rpc = PEER_SLAB_ROWS
    sps = PEER_SLABS_PER_STEP
    slot_new = b % 2
    slot_cur = 1 - slot_new
    first = b == 0
    last = b == pl.num_programs(0) - 1
    middle = jnp.logical_not(jnp.logical_or(first, last))

    def select_step():
        part = s % PEER_STEPS_PER_TILE
        rows = None
        for k in range(PEER_HEADS_PER_STEP):
            s0, s1 = st_ref[2 * k], st_ref[2 * k + 1]
            if rows is not None:
                anchor = 0.0 * rows[2][0:1, :]
                s0, s1 = s0 + anchor, s1 + anchor
            rows = _select_head(s0, s1)
            r0 = pl.multiple_of(part * (PEER_HEADS_PER_STEP * PK_TOPK), PEER_HEADS_PER_STEP * PK_TOPK)
            for x, val in enumerate(rows):
                stage_ref[x, pl.ds(r0 + k * PK_TOPK, PK_TOPK), :] = val

    def scores_pass():
        hc = _nt_dot(hn_ref[...], uv_ref[...])
        for k in range(sps):
            slab = hs_ref.at[s * sps + k]
            for ib in range(rpc):
                col = (k * rpc + ib) * N_KEYS
                slab[pl.ds(ib, tb, stride=rpc), :] = hc[:, col:col + N_KEYS]

    def combine_pass():
        c = s - PEER_NC
        wc = jnp.concatenate([hs_ref.at[c * sps + k][pl.ds(ib, tb, stride=rpc), :]
                              for k in range(sps) for ib in range(rpc)], axis=1)
        o_ref[...] += jnp.dot(wc.astype(BF16), uv_ref[...], preferred_element_type=F32)

    @pl.when(first)
    def _():
        select_step()

    @pl.when(jnp.logical_and(s < PEER_NC, middle))
    def _():
        scores_pass()
        select_step()

    @pl.when(jnp.logical_and(s < PEER_NC, last))
    def _():
        scores_pass()

    @pl.when(jnp.logical_and(s == PEER_NC - 1, jnp.logical_not(first)))
    def _weights():
        sub = lax.broadcasted_iota(jnp.int32, (N_KEYS, N_KEYS), 0).astype(F32).astype(BF16)
        one = jnp.ones((N_KEYS, N_KEYS), BF16)
        zero = jnp.zeros((N_KEYS, N_KEYS), BF16)

        def gate_matrix(n):
            irow = sel_ref[slot_cur, 0, pl.ds(n, 1), :].astype(BF16)
            jrow = sel_ref[slot_cur, 1, pl.ds(n, 1), :].astype(BF16)
            g_hi, g_lo = _split_bf16(0.5 * sel_ref[slot_cur, 2, pl.ds(n, 1), :])
            at = jnp.where(sub == irow, one, zero)
            jhit = sub == jrow
            b_hi = jnp.where(jhit, jnp.broadcast_to(g_hi, jhit.shape), zero)
            b_lo = jnp.where(jhit, jnp.broadcast_to(g_lo, jhit.shape), zero)
            return _nt_dot(jnp.concatenate([at, at], axis=1),
                           jnp.concatenate([b_hi, b_lo], axis=1))

        unroll = PEER_TOKEN_UNROLL
        n_groups = tb // unroll

        def make_gates(group, slot):
            group = jnp.minimum(group, n_groups - 1)
            for t in range(unroll):
                gm_ref[slot, t * N_KEYS:(t + 1) * N_KEYS, :] = gate_matrix(group * unroll + t)

        def apply_gates(group, slot):
            base = pl.multiple_of(group * (unroll * rpc), unroll * rpc)
            for c in range(PEER_N_SLABS):
                hv = hs_ref[c, pl.ds(base, unroll * rpc), :]
                gm = jnp.concatenate([gm_ref[slot, t * N_KEYS + c * rpc:t * N_KEYS + (c + 1) * rpc, :]
                                      for t in range(unroll)], axis=0)
                hs_ref[c, pl.ds(base, unroll * rpc), :] = (hv * gm) * (1.0 + lax.erf(hv * (2.0 ** -0.5)))

        def body(m, carry):
            apply_gates(2 * m, 0)
            make_gates(2 * m + 1, 1)
            apply_gates(2 * m + 1, 1)
            make_gates(2 * m + 2, 0)
            return carry

        o_ref[...] = h_ref[...]
        make_gates(0, 0)
        lax.fori_loop(0, n_groups // 2, body, 0)

    @pl.when(jnp.logical_and(s >= PEER_NC, middle))
    def _():
        combine_pass()
        select_step()

    @pl.when(jnp.logical_and(s >= PEER_NC, last))
    def _():
        combine_pass()

    @pl.when(jnp.logical_and(s % PEER_STEPS_PER_TILE == PEER_STEPS_PER_TILE - 1, jnp.logical_not(last)))
    def _publish():
        t0 = pl.multiple_of((s // PEER_STEPS_PER_TILE) * TOPK_TT, TOPK_TT)
        for x in range(3):
            sel_ref[slot_new, x, pl.ds(t0, TOPK_TT), :] = stage_ref[x].T


def _cast_tables_kernel(u_ref, v_ref, o_ref):
    o_ref[0] = u_ref[...].astype(BF16)
    o_ref[1] = v_ref[...].astype(BF16)


def _cast_tables(u, v):
    depth, e, d = u.shape
    rc = PEER_EC // 2
    src = pl.BlockSpec((None, rc, d), lambda l, r: (l, r, 0))
    return pl.pallas_call(
        _cast_tables_kernel,
        grid=(depth, e // rc),
        in_specs=[src, src],
        out_specs=pl.BlockSpec((None, 2, rc, d), lambda l, r: (l, 0, r, 0)),
        out_shape=jax.ShapeDtypeStruct((depth, 2, e, d), BF16),
        compiler_params=_cparams(("arbitrary", "arbitrary")),
        name="cast_tables",
    )(u, v)


def _peer(hn, h, st, uv, layer):
    n = hn.shape[0]
    tb = PEER_TB
    nb = n // tb
    nc = PEER_NC
    tiles = tb // TOPK_TT
    row = lambda w: pl.BlockSpec((tb, w), lambda b, s: (jnp.maximum(b - 1, 0), 0))
    st_rows = 2 * PEER_HEADS_PER_STEP
    st_spec = pl.BlockSpec(
        (st_rows, N_KEYS, TOPK_TT),
        lambda b, s: (s % PEER_STEPS_PER_TILE, 0, jnp.minimum(b, nb - 1) * tiles + s // PEER_STEPS_PER_TILE))
    return pl.pallas_call(
        _peer_kernel,
        grid=(nb + 1, 2 * nc),
        in_specs=[row(D_MODEL), row(D_MODEL), st_spec,
                  pl.BlockSpec((None, None, PEER_EC, D_MODEL), lambda b, s: (layer, s // nc, s % nc, 0))],
        out_specs=row(D_MODEL),
        out_shape=jax.ShapeDtypeStruct((n, D_MODEL), F32),
        scratch_shapes=[pltpu.VMEM((PEER_N_SLABS, tb * PEER_SLAB_ROWS, N_KEYS), F32),
                        pltpu.VMEM((2, PEER_TOKEN_UNROLL * N_KEYS, N_KEYS), F32),
                        pltpu.VMEM((2, 3, tb, PEER_NSEL), F32),
                        pltpu.VMEM((3, PEER_NSEL, TOPK_TT), F32)],
        compiler_params=pltpu.CompilerParams(dimension_semantics=("arbitrary", "arbitrary"),
                                             vmem_limit_bytes=PEER_VMEM_LIMIT),
        name="peer_experts",
    )(hn, h, st, uv)


def _bias_kernel(g_ref, full_ref, band_ref):
    _, nq, nk = full_ref.shape
    w = g_ref.shape[1]
    qc = lax.broadcasted_iota(jnp.int32, (nq, nk), 0) // CHUNK
    kc = lax.broadcasted_iota(jnp.int32, (nq, nk), 1) // CHUNK
    band = jnp.abs(2 * (kc - qc) - BAND // CHUNK) <= BAND // CHUNK
    for h in range(ATT_HEADS):
        line = jnp.broadcast_to(g_ref[h:h + 1, :], (nq, w))
        t = pltpu.roll(line, w - nq + 1, 1, stride=1, stride_axis=0)[:, :nk]
        full_ref[h] = t
        band_ref[h] = jnp.where(band, t, NEG_BIG)


def _rel_bias(table, nq, nk):
    w = 1024
    assert nq + nk - 1 <= w
    dist = nk - 1 - jnp.arange(w)
    line = table[:, jnp.clip(dist, -REL_CLIP, REL_CLIP) + REL_CLIP].astype(F32)
    shape = jax.ShapeDtypeStruct((ATT_HEADS, nq, nk), F32)
    return pl.pallas_call(
        _bias_kernel,
        out_shape=[shape, shape],
        compiler_params=pltpu.CompilerParams(vmem_limit_bytes=VMEM_LIMIT),
        name="rel_bias",
    )(line)


def _block_diag(blocks):
    g, r, c = blocks.shape
    eye = jnp.eye(g, dtype=blocks.dtype)
    return (eye[:, None, :, None] * blocks[:, :, None, :]).reshape(g * r, g * c)


def _sgu_mats(w_s, b_s, length):
    g = w_s.shape[0]
    reps = SGU_CHUNK // length
    tri = jnp.tril(jnp.ones((length, length), bool))
    ws = jnp.where(tri[None], w_s[:, :length, :length], 0.0)
    eye = jnp.eye(reps, dtype=ws.dtype)
    mats = (eye[None, :, None, :, None] * ws[:, None, :, None, :]).reshape(g, SGU_CHUNK, SGU_CHUNK)
    bias = jnp.tile(b_s[:, :length], (1, reps))
    bias = jnp.repeat(bias.T, SGU_GW, axis=1)
    return mats, bias


def kernel(x_prompt, x_sample, state_pool, cache_k, cache_v, g_mix, w_in, pool_w, pool_scale, sgu_norm, sgu_w, sgu_b, q_norm, k_norm, rel_table, w_gate, b_gate, w_br_a, w_br_b, w_br_c, w_out, g_ffn, peer_wq, peer_subkeys, peer_u, peer_v):
    depth = g_mix.shape[0]
    bsz, seq, d = x_prompt.shape
    dbsz, dseq, _ = x_sample.shape
    n_p, n_s = bsz * seq, dbsz * dseq
    n_cache = cache_k.shape[2]
    assert seq % ATT_QB == 0 and n_p % ROW_TILE == 0 and n_s % ROW_TILE == 0
    assert dseq <= SGU_CHUNK and SGU_CHUNK % dseq == 0 and dseq >= POOL_STATE + 1 and dseq % SUBLANES == 0
    assert n_cache == 2 * ATT_QB and dseq <= ATT_QB
    n_tot = -(-(n_p + n_s) // PEER_TB) * PEER_TB
    n_prompt_tiles = n_p // ROW_TILE
    xa, xb, xb_tile0 = x_prompt.reshape(n_p, d), x_sample.reshape(n_s, d), 0

    hsum = _block_diag(jnp.full((ATT_HEADS, HEAD_DIM, HEAD_DIM), 1.0 / HEAD_DIM, F32)).astype(BF16)
    zero_prefix = jnp.zeros((bsz, HALO, POOL_W), F32)
    uv_tables = _cast_tables(peer_u, peer_v)
    keep = min(BAND, seq)
    outs = {name: [] for name in ("pool_p", "pool_s", "k_p", "v_p", "k_s", "v_s", "sgu_s")}

    for l in range(depth):
        row = lambda a: a[l].reshape(1, -1)
        m_p, b_p = _sgu_mats(sgu_w[l], sgu_b[l], SGU_CHUNK)
        m_s, b_s = _sgu_mats(sgu_w[l], sgu_b[l], dseq)
        ms = jnp.stack([m_p, m_s]).astype(BF16)
        bs = jnp.stack([b_p, b_s])
        a, yb, vn, q, k, v = _in_proj(
            xa, xb, xb_tile0, n_tot, row(g_mix), w_in[l].astype(BF16), row(sgu_norm),
            jnp.tile(q_norm[l], ATT_HEADS).reshape(1, -1), jnp.tile(k_norm[l], ATT_HEADS).reshape(1, -1),
            hsum, ms, bs, n_prompt_tiles)

        pw = _block_diag(pool_w[l]).astype(BF16)
        sample_prefix = jnp.pad(state_pool[l], ((0, 0), (HALO - POOL_STATE, 0), (0, 0)))
        ya_p = _pool(a, zero_prefix, pw, row(pool_scale), 0, bsz, seq, ROW_TILE)
        ya_s = _pool(a, sample_prefix, pw, row(pool_scale), n_p, dbsz, dseq, dseq)

        bias_full, bias_band = _rel_bias(rel_table[l], ATT_QB, 3 * ATT_QB)
        yc_p = _attn_prompt(q, k, v, bias_band, bsz, seq)
        yc_s = _attn_sample(q, k, v, cache_k[l].reshape(dbsz, n_cache, ATT_W),
                            cache_v[l].reshape(dbsz, n_cache, ATT_W),
                            bias_full[:, :dseq, :n_cache + dseq], n_p, dbsz, dseq)

        h, hn, st = _merge(
            xa, xb, xb_tile0, n_tot, ya_p, ya_s, yb, yc_p, yc_s,
            row(g_mix), w_gate[l].astype(BF16), row(b_gate), w_br_a[l].astype(BF16),
            w_br_b[l].astype(BF16), w_br_c[l].astype(BF16), w_out[l].astype(BF16), row(g_ffn),
            peer_wq[l].astype(BF16), peer_subkeys[l].astype(BF16))
        x = _peer(hn, h, st, uv_tables, l)
        xa, xb, xb_tile0 = x, x, n_prompt_tiles

        def tail(arr, count):
            return jnp.stack([arr[(b + 1) * seq - count:(b + 1) * seq] for b in range(bsz)])

        sample = lambda arr: arr[n_p:n_p + n_s]
        outs["pool_p"].append(tail(a, POOL_STATE))
        outs["pool_s"].append(sample(a).reshape(dbsz, dseq, POOL_W)[:, dseq - POOL_STATE:])
        outs["k_p"].append(tail(k, keep).reshape(bsz, keep, ATT_HEADS, HEAD_DIM))
        outs["v_p"].append(tail(v, keep).reshape(bsz, keep, ATT_HEADS, HEAD_DIM))
        outs["k_s"].append(sample(k).reshape(dbsz, dseq, ATT_HEADS, HEAD_DIM))
        outs["v_s"].append(sample(v).reshape(dbsz, dseq, ATT_HEADS, HEAD_DIM))
        outs["sgu_s"].append(sample(vn).reshape(dbsz, dseq, SGU_W))

    st = lambda name: jnp.stack(outs[name])
    return (x[:n_p].reshape(bsz, seq, d), x[n_p:n_p + n_s].reshape(dbsz, dseq, d), st("pool_p"), st("pool_s"),
            st("k_p"), st("v_p"), st("k_s"), st("v_s"), st("sgu_s"))
```

```python
import functools

import jax
import jax.numpy as jnp
from jax import lax
from jax.experimental import pallas as pl
from jax.experimental.pallas import tpu as pltpu

F32 = jnp.float32
BF16 = jnp.bfloat16

EPS = 1e-6
D_MODEL = 1024
POOL_W = 256
POOL_GW = 64
POOL_STATE = 15
SGU_W = 256
SGU_GW = 64
SGU_CHUNK = 128
ATT_HEADS = 8
HEAD_DIM = 64
ATT_W = ATT_HEADS * HEAD_DIM
CHUNK = 64
BAND = 512
REL_CLIP = 128
N_BRANCH = 3
PEER_HEADS = 8
N_KEYS = 128
PK_HALF = 128
PK_TOPK = 16
NEG_BIG = -1e30

LANES = 128
SUBLANES = 8
ROW_TILE = 256
ATT_QB = 256
HALO = 16
TOPK_TT = 128
PEER_SLAB_ROWS = SUBLANES
PEER_N_SLABS = N_KEYS // PEER_SLAB_ROWS
PEER_SLABS_PER_STEP = 2
PEER_EC = PEER_SLABS_PER_STEP * PEER_SLAB_ROWS * N_KEYS
PEER_NC = N_KEYS * N_KEYS // PEER_EC
PEER_TOKEN_UNROLL = 16
PEER_TB = 512
VMEM_LIMIT = 56 * 1024 * 1024
PEER_VMEM_LIMIT = 60 * 1024 * 1024


def _cparams(sem):
    return pltpu.CompilerParams(dimension_semantics=sem, vmem_limit_bytes=VMEM_LIMIT)


def _const_spec(shape):
    n = len(shape)
    return pl.BlockSpec(shape, lambda *_: (0,) * n)


def _nt_dot(a, b):
    return lax.dot_general(a, b, (((1,), (1,)), ((), ())), preferred_element_type=F32)


def _split_bf16(x):
    hi = x.astype(BF16)
    lo = (x - hi.astype(F32)).astype(BF16)
    return hi, lo


def _row_specs(xa, xb, xb_tile0, n_prompt_tiles, width):
    tm = ROW_TILE
    last_b = xb.shape[0] // tm - 1
    return [pl.BlockSpec((tm, width), lambda i: (jnp.minimum(i, n_prompt_tiles - 1), 0)),
            pl.BlockSpec((tm, width), lambda i: (jnp.clip(xb_tile0 + i - n_prompt_tiles, xb_tile0, last_b), 0))]


def _in_proj_kernel(xa_ref, xb_ref, gmix_ref, win_ref, sgn_ref, qn_ref, kn_ref, hsum_ref, ms_ref, bs_ref,
                    a_ref, yb_ref, vn_ref, q_ref, k_ref, v_ref, *, n_prompt_tiles):
    x = jnp.where(pl.program_id(0) < n_prompt_tiles, xa_ref[...], xb_ref[...])
    xn = x * lax.rsqrt(jnp.mean(x * x, axis=-1, keepdims=True) + EPS) * gmix_ref[...]
    z = jnp.dot(xn.astype(BF16), win_ref[...], preferred_element_type=F32)
    a_ref[...] = z[:, 0:POOL_W]
    u = z[:, POOL_W:POOL_W + SGU_W]
    vb = z[:, POOL_W + SGU_W:POOL_W + 2 * SGU_W]
    vn = vb * lax.rsqrt(jnp.mean(vb * vb, axis=-1, keepdims=True) + EPS) * sgn_ref[...]
    vn_ref[...] = vn
    vnb = vn.astype(BF16)
    lane_group = lax.broadcasted_iota(jnp.int32, (SGU_CHUNK, SGU_W), 1) // SGU_GW
    rows = x.shape[0]
    for c in range(rows // SGU_CHUNK):
        sl = slice(c * SGU_CHUNK, (c + 1) * SGU_CHUNK)
        vc = vnb[sl, :]
        mixed = jnp.dot(ms_ref[0, 0], vc, preferred_element_type=F32)
        for g in range(1, SGU_W // SGU_GW):
            mg = jnp.dot(ms_ref[0, g], vc, preferred_element_type=F32)
            mixed = jnp.where(lane_group == g, mg, mixed)
        yb_ref[sl, :] = (u[sl, :] * (mixed + bs_ref[0])).astype(BF16)

    base = POOL_W + 2 * SGU_W
    q = z[:, base:base + ATT_W]
    k = z[:, base + ATT_W:base + 2 * ATT_W]
    v_ref[...] = z[:, base + 2 * ATT_W:base + 3 * ATT_W]

    def head_norm(t, w_ref):
        m = jnp.dot((t * t).astype(BF16), hsum_ref[...], preferred_element_type=F32)
        return t * lax.rsqrt(m + EPS) * w_ref[...]

    q_ref[...] = (head_norm(q, qn_ref) * (HEAD_DIM ** -0.5)).astype(BF16)
    k_ref[...] = head_norm(k, kn_ref)


def _in_proj(xa, xb, xb_tile0, n, gmix, win, sgn, qn, kn, hsum, ms, bs, n_prompt_tiles):
    tm = ROW_TILE
    in_w = win.shape[1]
    row = lambda w: pl.BlockSpec((tm, w), lambda i: (i, 0))
    sel = lambda i: (jnp.where(i >= n_prompt_tiles, 1, 0), 0, 0, 0)
    sel3 = lambda i: (jnp.where(i >= n_prompt_tiles, 1, 0), 0, 0)
    return pl.pallas_call(
        functools.partial(_in_proj_kernel, n_prompt_tiles=n_prompt_tiles),
        grid=(n // tm,),
        in_specs=_row_specs(xa, xb, xb_tile0, n_prompt_tiles, D_MODEL) + [
                  _const_spec((1, D_MODEL)), _const_spec((D_MODEL, in_w)),
                  _const_spec((1, SGU_W)), _const_spec((1, ATT_W)), _const_spec((1, ATT_W)),
                  _const_spec((ATT_W, ATT_W)),
                  pl.BlockSpec((1, SGU_W // SGU_GW, SGU_CHUNK, SGU_CHUNK), sel),
                  pl.BlockSpec((1, SGU_CHUNK, SGU_W), sel3)],
        out_specs=[row(POOL_W), row(SGU_W), row(SGU_W), row(ATT_W), row(ATT_W), row(ATT_W)],
        out_shape=[jax.ShapeDtypeStruct((n, POOL_W), F32), jax.ShapeDtypeStruct((n, SGU_W), BF16),
                   jax.ShapeDtypeStruct((n, SGU_W), F32), jax.ShapeDtypeStruct((n, ATT_W), BF16),
                   jax.ShapeDtypeStruct((n, ATT_W), F32), jax.ShapeDtypeStruct((n, ATT_W), F32)],
        compiler_params=_cparams(("arbitrary",)),
        name="in_proj",
    )(xa, xb, gmix, win, sgn, qn, kn, hsum, ms, bs)


def _pool_kernel(a_ref, prev_ref, pre_ref, pw_ref, sc_ref, y_ref):
    t = pl.program_id(1)
    a = a_ref[...]
    tm = a.shape[0]
    halo = jnp.where(t == 0, pre_ref[0], prev_ref[...])
    e = jnp.concatenate([halo, a], axis=0)
    s2 = e[1:] + e[:-1]
    s4 = s2[2:] + s2[:-2]
    s8 = s4[4:] + s4[:-4]
    s16 = s8[8:] + s8[:-8]
    lg = lax.broadcasted_iota(jnp.int32, (tm, POOL_W), 1) // POOL_GW
    mean = jnp.where(lg == 0, s2[15:] * 0.5,
                     jnp.where(lg == 1, s4[13:] * 0.25,
                               jnp.where(lg == 2, s8[9:] * 0.125, s16[1:] * 0.0625)))
    pooled = mean - a
    y = jnp.dot(pooled.astype(BF16), pw_ref[...], preferred_element_type=F32) * sc_ref[...]
    y_ref[...] = y.astype(BF16)


def _pool(a, prefix, pw, sc, row0, n_streams, t_len, tm):
    nt = t_len // tm
    b0 = row0 // tm
    hb = tm // HALO
    return pl.pallas_call(
        _pool_kernel,
        grid=(n_streams, nt),
        in_specs=[pl.BlockSpec((tm, POOL_W), lambda b, t: (b0 + b * nt + t, 0)),
                  pl.BlockSpec((HALO, POOL_W), lambda b, t: (jnp.maximum((b0 + b * nt + t) * hb - 1, 0), 0)),
                  pl.BlockSpec((1, HALO, POOL_W), lambda b, t: (b, 0, 0)),
                  _const_spec((POOL_W, POOL_W)), _const_spec((1, POOL_W))],
        out_specs=pl.BlockSpec((tm, POOL_W), lambda b, t: (b * nt + t, 0)),
        out_shape=jax.ShapeDtypeStruct((n_streams * t_len, POOL_W), BF16),
        compiler_params=_cparams(("arbitrary", "arbitrary")),
        name="pool_mix",
    )(a, a, prefix, pw, sc)


def _attend(q, k, v, bias_ref, col_bias, o_ref):
    qb = q.shape[0]
    lane_hi = lax.broadcasted_iota(jnp.int32, (qb, 2 * HEAD_DIM), 1) >= HEAD_DIM
    for hp in range(ATT_HEADS // 2):
        sl = slice(hp * 2 * HEAD_DIM, (hp + 1) * 2 * HEAD_DIM)
        q2, k2, v2 = q[:, sl], k[:, sl], v[:, sl]
        out = None
        for sub in range(2):
            qm = jnp.where(lane_hi == (sub == 1), q2, jnp.zeros_like(q2))
            s = _nt_dot(qm, k2) + bias_ref[hp * 2 + sub]
            if col_bias is not None:
                s = s + col_bias
            m = jnp.max(s, axis=-1, keepdims=True)
            p = jnp.exp(s - m)
            l = jnp.sum(p, axis=-1, keepdims=True)
            o = jnp.dot(p.astype(BF16), v2, preferred_element_type=F32) / l
            out = o if sub == 0 else jnp.where(lane_hi, o, out)
        o_ref[:, sl] = out.astype(BF16)


def _attn_prompt_kernel(q_ref, k0_ref, k1_ref, k2_ref, v0_ref, v1_ref, v2_ref, bias_ref, o_ref):
    t = pl.program_id(1)
    qb = q_ref.shape[0]
    k = jnp.concatenate([k0_ref[...], k1_ref[...], k2_ref[...]], axis=0).astype(BF16)
    v = jnp.concatenate([v0_ref[...], v1_ref[...], v2_ref[...]], axis=0).astype(BF16)
    key_row = (t - 2) * qb + lax.broadcasted_iota(jnp.int32, (1, 3 * qb), 1)
    col_bias = jnp.where(key_row >= 0, 0.0, NEG_BIG).astype(F32)
    _attend(q_ref[...], k, v, bias_ref, col_bias, o_ref)


def _attn_prompt(q, k, v, bias, n_streams, t_len):
    qb = ATT_QB
    nt = t_len // qb
    cur = lambda b, t: (b * nt + t, 0)
    prev1 = lambda b, t: (b * nt + jnp.maximum(t - 1, 0), 0)
    prev2 = lambda b, t: (b * nt + jnp.maximum(t - 2, 0), 0)
    blk = lambda im: pl.BlockSpec((qb, ATT_W), im)
    return pl.pallas_call(
        _attn_prompt_kernel,
        grid=(n_streams, nt),
        in_specs=[blk(cur), blk(prev2), blk(prev1), blk(cur), blk(prev2), blk(prev1), blk(cur),
                  _const_spec(bias.shape)],
        out_specs=blk(cur),
        out_shape=jax.ShapeDtypeStruct((n_streams * t_len, ATT_W), BF16),
        compiler_params=_cparams(("arbitrary", "arbitrary")),
        name="attn_prompt",
    )(q, k, k, k, v, v, v, bias)


def _attn_sample_kernel(q_ref, kn_ref, vn_ref, kc_ref, vc_ref, bias_ref, o_ref):
    k = jnp.concatenate([kc_ref[0], kn_ref[...]], axis=0).astype(BF16)
    v = jnp.concatenate([vc_ref[0], vn_ref[...]], axis=0).astype(BF16)
    _attend(q_ref[...], k, v, bias_ref, None, o_ref)


def _attn_sample(q, k, v, kc, vc, bias, row0, n_streams, s_len):
    b0 = row0 // s_len
    n_cache = kc.shape[1]
    new = pl.BlockSpec((s_len, ATT_W), lambda s: (b0 + s, 0))
    cache = pl.BlockSpec((1, n_cache, ATT_W), lambda s: (s, 0, 0))
    return pl.pallas_call(
        _attn_sample_kernel,
        grid=(n_streams,),
        in_specs=[new, new, new, cache, cache, _const_spec(bias.shape)],
        out_specs=pl.BlockSpec((s_len, ATT_W), lambda s: (s, 0)),
        out_shape=jax.ShapeDtypeStruct((n_streams * s_len, ATT_W), BF16),
        compiler_params=_cparams(("arbitrary",)),
        name="attn_sample",
    )(q, k, v, kc, vc, bias)


def _merge_kernel(xa_ref, xb_ref, yap_ref, yas_ref, yb_ref, ycp_ref, ycs_ref, gmix_ref, wg_ref, bg_ref, wa_ref,
                  wb_ref, wc_ref, wo_ref, gffn_ref, wq_ref, sk_ref, h_ref, hn_ref, st_ref, *, n_prompt_tiles):
    is_prompt = pl.program_id(0) < n_prompt_tiles
    x = jnp.where(is_prompt, xa_ref[...], xb_ref[...])
    xn = (x * lax.rsqrt(jnp.mean(x * x, axis=-1, keepdims=True) + EPS) * gmix_ref[...]).astype(BF16)
    ya = jnp.where(is_prompt, yap_ref[...], yas_ref[...])
    yc = jnp.where(is_prompt, ycp_ref[...], ycs_ref[...])
    merged = None
    for b, (y, w_ref) in enumerate(((ya, wa_ref), (yb_ref[...], wb_ref), (yc, wc_ref))):
        cols = slice(b * D_MODEL, (b + 1) * D_MODEL)
        gate = jax.nn.sigmoid(jnp.dot(xn, wg_ref[:, cols], preferred_element_type=F32) + bg_ref[:, cols])
        term = gate * jnp.dot(y, w_ref[...], preferred_element_type=F32)
        merged = term if merged is None else merged + term
    h = x + jnp.dot(merged.astype(BF16), wo_ref[...], preferred_element_type=F32)
    h_ref[...] = h
    hn = (h * lax.rsqrt(jnp.mean(h * h, axis=-1, keepdims=True) + EPS) * gffn_ref[...]).astype(BF16)
    hn_ref[...] = hn
    qp = jnp.dot(hn, wq_ref[...], preferred_element_type=F32).astype(BF16)
    for hp in range(2 * PEER_HEADS):
        st_ref[hp] = _nt_dot(sk_ref[hp % 2], qp[:, hp * PK_HALF:(hp + 1) * PK_HALF])


def _merge(xa, xb, xb_tile0, n, ya_p, ya_s, yb, yc_p, yc_s, gmix, wg, bg, wa, wb, wc, wo, gffn, wq, sk):
    tm = ROW_TILE
    npt, nst = ya_p.shape[0] // tm, ya_s.shape[0] // tm
    row = lambda w: pl.BlockSpec((tm, w), lambda i: (i, 0))
    prow = lambda w: pl.BlockSpec((tm, w), lambda i: (jnp.minimum(i, npt - 1), 0))
    srow = lambda w: pl.BlockSpec((tm, w), lambda i: (jnp.clip(i - npt, 0, nst - 1), 0))
    return pl.pallas_call(
        functools.partial(_merge_kernel, n_prompt_tiles=npt),
        grid=(n // tm,),
        in_specs=_row_specs(xa, xb, xb_tile0, npt, D_MODEL) + [
                  prow(POOL_W), srow(POOL_W), row(SGU_W), prow(ATT_W), srow(ATT_W),
                  _const_spec((1, D_MODEL)), _const_spec(wg.shape), _const_spec(bg.shape),
                  _const_spec(wa.shape), _const_spec(wb.shape), _const_spec(wc.shape),
                  _const_spec(wo.shape), _const_spec((1, D_MODEL)), _const_spec(wq.shape),
                  _const_spec(sk.shape)],
        out_specs=[row(D_MODEL), row(D_MODEL),
                   pl.BlockSpec((2 * PEER_HEADS, N_KEYS, tm), lambda i: (0, 0, i))],
        out_shape=[jax.ShapeDtypeStruct((n, D_MODEL), F32), jax.ShapeDtypeStruct((n, D_MODEL), BF16),
                   jax.ShapeDtypeStruct((2 * PEER_HEADS, N_KEYS, n), F32)],
        compiler_params=_cparams(("arbitrary",)),
        name="merge",
    )(xa, xb, ya_p, ya_s, yb, yc_p, yc_s, gmix, wg, bg, wa, wb, wc, wo, gffn, wq, sk)


def _top16(x, iota, payloads=()):
    n = x.shape[0]
    vals, idxs = [], []
    picked = [[] for _ in payloads]
    for _ in range(PK_TOPK):
        m = jnp.max(x, axis=0, keepdims=True)
        idx = jnp.min(jnp.where(x == m, iota, float(n)), axis=0, keepdims=True)
        hit = iota == idx
        vals.append(m)
        idxs.append(idx)
        for lst, p in zip(picked, payloads):
            lst.append(jnp.sum(jnp.where(hit, p, 0.0), axis=0, keepdims=True))
        x = jnp.where(hit, -jnp.inf, x)
    return vals, idxs, picked


_KEY_LISTS = 4


def _top16_keys(x):
    n, tt = x.shape
    rows = n // _KEY_LISTS
    base = lax.broadcasted_iota(jnp.int32, (rows, tt), 0).astype(F32)
    val = [x[l * rows:(l + 1) * rows] for l in range(_KEY_LISTS)]
    idx = [base + float(l * rows) for l in range(_KEY_LISTS)]
    for a in [p for end in range(_KEY_LISTS - 1, 0, -1) for p in range(end)]:
        swap = val[a] < val[a + 1]
        val[a], val[a + 1] = jnp.where(swap, val[a + 1], val[a]), jnp.where(swap, val[a], val[a + 1])
        idx[a], idx[a + 1] = jnp.where(swap, idx[a + 1], idx[a]), jnp.where(swap, idx[a], idx[a + 1])
    vals, idxs = [], []
    for _ in range(PK_TOPK):
        m = jnp.max(val[0], axis=0, keepdims=True)
        sel = jnp.min(jnp.where(val[0] == m, idx[0], float(n)), axis=0, keepdims=True)
        hit = idx[0] == sel
        vals.append(m)
        idxs.append(sel)
        for l in range(_KEY_LISTS - 1):
            val[l] = jnp.where(hit, val[l + 1], val[l])
            idx[l] = jnp.where(hit, idx[l + 1], idx[l])
        val[-1] = jnp.where(hit, -jnp.inf, val[-1])
    return vals, idxs


_PAIRS = [(a, b) for a in range(PK_TOPK) for b in range(PK_TOPK) if (a + 1) * (b + 1) <= PK_TOPK]
_PAIR_ROWS = -(-len(_PAIRS) // SUBLANES) * SUBLANES


def _select_head(s0, s1):
    tt = s0.shape[1]
    iota_c = lax.broadcasted_iota(jnp.int32, (_PAIR_ROWS, tt), 0).astype(F32)
    pad = _PAIR_ROWS - len(_PAIRS)
    neg_row = jnp.full((1, tt), -jnp.inf, F32)
    zero_row = jnp.zeros((1, tt), F32)
    v0, i0 = _top16_keys(s0)
    v1, i1 = _top16_keys(s1)
    cand = jnp.concatenate([v0[a] + v1[b] for a, b in _PAIRS] + [neg_row] * pad, axis=0)
    cand_i = jnp.concatenate([i0[a] for a, _ in _PAIRS] + [zero_row] * pad, axis=0)
    cand_j = jnp.concatenate([i1[b] for _, b in _PAIRS] + [zero_row] * pad, axis=0)
    tv, _, (ti, tj) = _top16(cand, iota_c, (cand_i, cand_j))
    e = jnp.exp(jnp.concatenate(tv, axis=0) - tv[0])
    gates = e / jnp.sum(e, axis=0, keepdims=True)
    return jnp.concatenate(ti, axis=0), jnp.concatenate(tj, axis=0), gates


PEER_STEPS_PER_TILE = 2 * PEER_NC // (PEER_TB // TOPK_TT)
PEER_HEADS_PER_STEP = PEER_HEADS // PEER_STEPS_PER_TILE
PEER_NSEL = PEER_HEADS * PK_TOPK


def _peer_kernel(hn_ref, h_ref, st_ref, uv_ref, o_ref, hs_ref, gm_ref, sel_ref, stage_ref):
    b = pl.program_id(0)
    s = pl.program_id(1)
    tb = hn_ref.shape[0]
    rpc = PEER_SLAB_ROWS
    sps = PEER_SLABS_PER_STEP
    slot_new = b % 2
    slot_cur = 1 - slot_new
    first = b == 0
    last = b == pl.num_programs(0) - 1
    middle = jnp.logical_not(jnp.logical_or(first, last))

    def select_step():
        part = s % PEER_STEPS_PER_TILE
        rows = None
        for k in range(PEER_HEADS_PER_STEP):
            s0, s1 = st_ref[2 * k], st_ref[2 * k + 1]
            if rows is not None:
                anchor = 0.0 * rows[2][0:1, :]
                s0, s1 = s0 + anchor, s1 + anchor
            rows = _select_head(s0, s1)
            r0 = pl.multiple_of(part * (PEER_HEADS_PER_STEP * PK_TOPK), PEER_HEADS_PER_STEP * PK_TOPK)
            for x, val in enumerate(rows):
                stage_ref[x, pl.ds(r0 + k * PK_TOPK, PK_TOPK), :] = val

    def scores_pass():
        hc = _nt_dot(hn_ref[...], uv_ref[...])
        for k in range(sps):
            slab = hs_ref.at[s * sps + k]
            for ib in range(rpc):
                col = (k * rpc + ib) * N_KEYS
                slab[pl.ds(ib, tb, stride=rpc), :] = hc[:, col:col + N_KEYS]

    def combine_pass():
        c = s - PEER_NC
        wc = jnp.concatenate([hs_ref.at[c * sps + k][pl.ds(ib, tb, stride=rpc), :]
                              for k in range(sps) for ib in range(rpc)], axis=1)
        o_ref[...] += jnp.dot(wc.astype(BF16), uv_ref[...], preferred_element_type=F32)

    @pl.when(first)
    def _():
        select_step()

    @pl.when(jnp.logical_and(s < PEER_NC, middle))
    def _():
        select_step()
        scores_pass()

    @pl.when(jnp.logical_and(s < PEER_NC, last))
    def _():
        scores_pass()

    @pl.when(jnp.logical_and(s == PEER_NC - 1, jnp.logical_not(first)))
    def _weights():
        sub = lax.broadcasted_iota(jnp.int32, (N_KEYS, N_KEYS), 0).astype(F32).astype(BF16)
        one = jnp.ones((N_KEYS, N_KEYS), BF16)
        zero = jnp.zeros((N_KEYS, N_KEYS), BF16)

        def gate_matrix(n):
            irow = sel_ref[slot_cur, 0, pl.ds(n, 1), :].astype(BF16)
            jrow = sel_ref[slot_cur, 1, pl.ds(n, 1), :].astype(BF16)
            g_hi, g_lo = _split_bf16(0.5 * sel_ref[slot_cur, 2, pl.ds(n, 1), :])
            at = jnp.where(sub == irow, one, zero)
            jhit = sub == jrow
            b_hi = jnp.where(jhit, jnp.broadcast_to(g_hi, jhit.shape), zero)
            b_lo = jnp.where(jhit, jnp.broadcast_to(g_lo, jhit.shape), zero)
            return _nt_dot(jnp.concatenate([at, at], axis=1),
                           jnp.concatenate([b_hi, b_lo], axis=1))

        unroll = PEER_TOKEN_UNROLL
        n_groups = tb // unroll

        def make_gates(group, slot):
            group = jnp.minimum(group, n_groups - 1)
            for t in range(unroll):
                gm_ref[slot, t * N_KEYS:(t + 1) * N_KEYS, :] = gate_matrix(group * unroll + t)

        def apply_gates(group, slot):
            base = pl.multiple_of(group * (unroll * rpc), unroll * rpc)
            for c in range(PEER_N_SLABS):
                hv = hs_ref[c, pl.ds(base, unroll * rpc), :]
                gm = jnp.concatenate([gm_ref[slot, t * N_KEYS + c * rpc:t * N_KEYS + (c + 1) * rpc, :]
                                      for t in range(unroll)], axis=0)
                hs_ref[c, pl.ds(base, unroll * rpc), :] = (hv * gm) * (1.0 + lax.erf(hv * (2.0 ** -0.5)))

        def body(m, carry):
            apply_gates(2 * m, 0)
            make_gates(2 * m + 1, 1)
            apply_gates(2 * m + 1, 1)
            make_gates(2 * m + 2, 0)
            return carry

        o_ref[...] = h_ref[...]
        make_gates(0, 0)
        lax.fori_loop(0, n_groups // 2, body, 0)

    @pl.when(jnp.logical_and(s >= PEER_NC, middle))
    def _():
        select_step()
        combine_pass()

    @pl.when(jnp.logical_and(s >= PEER_NC, last))
    def _():
        combine_pass()

    @pl.when(jnp.logical_and(s % PEER_STEPS_PER_TILE == PEER_STEPS_PER_TILE - 1, jnp.logical_not(last)))
    def _publish():
        t0 = pl.multiple_of((s // PEER_STEPS_PER_TILE) * TOPK_TT, TOPK_TT)
        for x in range(3):
            sel_ref[slot_new, x, pl.ds(t0, TOPK_TT), :] = stage_ref[x].T


def _cast_tables_kernel(u_ref, v_ref, o_ref):
    o_ref[0] = u_ref[...].astype(BF16)
    o_ref[1] = v_ref[...].astype(BF16)


def _cast_tables(u, v):
    depth, e, d = u.shape
    rc = PEER_EC // 2
    src = pl.BlockSpec((None, rc, d), lambda l, r: (l, r, 0))
    return pl.pallas_call(
        _cast_tables_kernel,
        grid=(depth, e // rc),
        in_specs=[src, src],
        out_specs=pl.BlockSpec((None, 2, rc, d), lambda l, r: (l, 0, r, 0)),
        out_shape=jax.ShapeDtypeStruct((depth, 2, e, d), BF16),
        compiler_params=_cparams(("arbitrary", "arbitrary")),
        name="cast_tables",
    )(u, v)


def _peer(hn, h, st, uv, layer):
    n = hn.shape[0]
    tb = PEER_TB
    nb = n // tb
    nc = PEER_NC
    tiles = tb // TOPK_TT
    row = lambda w: pl.BlockSpec((tb, w), lambda b, s: (jnp.maximum(b - 1, 0), 0))
    st_rows = 2 * PEER_HEADS_PER_STEP
    st_spec = pl.BlockSpec(
        (st_rows, N_KEYS, TOPK_TT),
        lambda b, s: (s % PEER_STEPS_PER_TILE, 0, jnp.minimum(b, nb - 1) * tiles + s // PEER_STEPS_PER_TILE))
    return pl.pallas_call(
        _peer_kernel,
        grid=(nb + 1, 2 * nc),
        in_specs=[row(D_MODEL), row(D_MODEL), st_spec,
                  pl.BlockSpec((None, None, PEER_EC, D_MODEL), lambda b, s: (layer, s // nc, s % nc, 0))],
        out_specs=row(D_MODEL),
        out_shape=jax.ShapeDtypeStruct((n, D_MODEL), F32),
        scratch_shapes=[pltpu.VMEM((PEER_N_SLABS, tb * PEER_SLAB_ROWS, N_KEYS), F32),
                        pltpu.VMEM((2, PEER_TOKEN_UNROLL * N_KEYS, N_KEYS), F32),
                        pltpu.VMEM((2, 3, tb, PEER_NSEL), F32),
                        pltpu.VMEM((3, PEER_NSEL, TOPK_TT), F32)],
        compiler_params=pltpu.CompilerParams(dimension_semantics=("arbitrary", "arbitrary"),
                                             vmem_limit_bytes=PEER_VMEM_LIMIT),
        name="peer_experts",
    )(hn, h, st, uv)


def _bias_kernel(g_ref, full_ref, band_ref):
    _, nq, nk = full_ref.shape
    w = g_ref.shape[1]
    qc = lax.broadcasted_iota(jnp.int32, (nq, nk), 0) // CHUNK
    kc = lax.broadcasted_iota(jnp.int32, (nq, nk), 1) // CHUNK
    band = jnp.abs(2 * (kc - qc) - BAND // CHUNK) <= BAND // CHUNK
    for h in range(ATT_HEADS):
        line = jnp.broadcast_to(g_ref[h:h + 1, :], (nq, w))
        t = pltpu.roll(line, w - nq + 1, 1, stride=1, stride_axis=0)[:, :nk]
        full_ref[h] = t
        band_ref[h] = jnp.where(band, t, NEG_BIG)


def _rel_bias(table, nq, nk):
    w = 1024
    assert nq + nk - 1 <= w
    dist = nk - 1 - jnp.arange(w)
    line = table[:, jnp.clip(dist, -REL_CLIP, REL_CLIP) + REL_CLIP].astype(F32)
    shape = jax.ShapeDtypeStruct((ATT_HEADS, nq, nk), F32)
    return pl.pallas_call(
        _bias_kernel,
        out_shape=[shape, shape],
        compiler_params=pltpu.CompilerParams(vmem_limit_bytes=VMEM_LIMIT),
        name="rel_bias",
    )(line)


def _block_diag(blocks):
    g, r, c = blocks.shape
    eye = jnp.eye(g, dtype=blocks.dtype)
    return (eye[:, None, :, None] * blocks[:, :, None, :]).reshape(g * r, g * c)


def _sgu_mats(w_s, b_s, length):
    g = w_s.shape[0]
    reps = SGU_CHUNK // length
    tri = jnp.tril(jnp.ones((length, length), bool))
    ws = jnp.where(tri[None], w_s[:, :length, :length], 0.0)
    eye = jnp.eye(reps, dtype=ws.dtype)
    mats = (eye[None, :, None, :, None] * ws[:, None, :, None, :]).reshape(g, SGU_CHUNK, SGU_CHUNK)
    bias = jnp.tile(b_s[:, :length], (1, reps))
    bias = jnp.repeat(bias.T, SGU_GW, axis=1)
    return mats, bias


def kernel(x_prompt, x_sample, state_pool, cache_k, cache_v, g_mix, w_in, pool_w, pool_scale, sgu_norm, sgu_w, sgu_b, q_norm, k_norm, rel_table, w_gate, b_gate, w_br_a, w_br_b, w_br_c, w_out, g_ffn, peer_wq, peer_subkeys, peer_u, peer_v):
    depth = g_mix.shape[0]
    bsz, seq, d = x_prompt.shape
    dbsz, dseq, _ = x_sample.shape
    n_p, n_s = bsz * seq, dbsz * dseq
    n_cache = cache_k.shape[2]
    assert seq % ATT_QB == 0 and n_p % ROW_TILE == 0 and n_s % ROW_TILE == 0
    assert dseq <= SGU_CHUNK and SGU_CHUNK % dseq == 0 and dseq >= POOL_STATE + 1 and dseq % SUBLANES == 0
    assert n_cache == 2 * ATT_QB and dseq <= ATT_QB
    n_tot = -(-(n_p + n_s) // PEER_TB) * PEER_TB
    n_prompt_tiles = n_p // ROW_TILE
    xa, xb, xb_tile0 = x_prompt.reshape(n_p, d), x_sample.reshape(n_s, d), 0

    hsum = _block_diag(jnp.full((ATT_HEADS, HEAD_DIM, HEAD_DIM), 1.0 / HEAD_DIM, F32)).astype(BF16)
    zero_prefix = jnp.zeros((bsz, HALO, POOL_W), F32)
    uv_tables = _cast_tables(peer_u, peer_v)
    keep = min(BAND, seq)
    outs = {name: [] for name in ("pool_p", "pool_s", "k_p", "v_p", "k_s", "v_s", "sgu_s")}

    for l in range(depth):
        row = lambda a: a[l].reshape(1, -1)
        m_p, b_p = _sgu_mats(sgu_w[l], sgu_b[l], SGU_CHUNK)
        m_s, b_s = _sgu_mats(sgu_w[l], sgu_b[l], dseq)
        ms = jnp.stack([m_p, m_s]).astype(BF16)
        bs = jnp.stack([b_p, b_s])
        a, yb, vn, q, k, v = _in_proj(
            xa, xb, xb_tile0, n_tot, row(g_mix), w_in[l].astype(BF16), row(sgu_norm),
            jnp.tile(q_norm[l], ATT_HEADS).reshape(1, -1), jnp.tile(k_norm[l], ATT_HEADS).reshape(1, -1),
            hsum, ms, bs, n_prompt_tiles)

        pw = _block_diag(pool_w[l]).astype(BF16)
        sample_prefix = jnp.pad(state_pool[l], ((0, 0), (HALO - POOL_STATE, 0), (0, 0)))
        ya_p = _pool(a, zero_prefix, pw, row(pool_scale), 0, bsz, seq, ROW_TILE)
        ya_s = _pool(a, sample_prefix, pw, row(pool_scale), n_p, dbsz, dseq, dseq)

        bias_full, bias_band = _rel_bias(rel_table[l], ATT_QB, 3 * ATT_QB)
        yc_p = _attn_prompt(q, k, v, bias_band, bsz, seq)
        yc_s = _attn_sample(q, k, v, cache_k[l].reshape(dbsz, n_cache, ATT_W),
                            cache_v[l].reshape(dbsz, n_cache, ATT_W),
                            bias_full[:, :dseq, :n_cache + dseq], n_p, dbsz, dseq)

        h, hn, st = _merge(
            xa, xb, xb_tile0, n_tot, ya_p, ya_s, yb, yc_p, yc_s,
            row(g_mix), w_gate[l].astype(BF16), row(b_gate), w_br_a[l].astype(BF16),
            w_br_b[l].astype(BF16), w_br_c[l].astype(BF16), w_out[l].astype(BF16), row(g_ffn),
            peer_wq[l].astype(BF16), peer_subkeys[l].astype(BF16))
        x = _peer(hn, h, st, uv_tables, l)
        xa, xb, xb_tile0 = x, x, n_prompt_tiles

        def tail(arr, count):
            return jnp.stack([arr[(b + 1) * seq - count:(b + 1) * seq] for b in range(bsz)])

        sample = lambda arr: arr[n_p:n_p + n_s]
        outs["pool_p"].append(tail(a, POOL_STATE))
        outs["pool_s"].append(sample(a).reshape(dbsz, dseq, POOL_W)[:, dseq - POOL_STATE:])
        outs["k_p"].append(tail(k, keep).reshape(bsz, keep, ATT_HEADS, HEAD_DIM))
        outs["v_p"].append(tail(v, keep).reshape(bsz, keep, ATT_HEADS, HEAD_DIM))
        outs["k_s"].append(sample(k).reshape(dbsz, dseq, ATT_HEADS, HEAD_DIM))
        outs["v_s"].append(sample(v).reshape(dbsz, dseq, ATT_HEADS, HEAD_DIM))
        outs["sgu_s"].append(sample(vn).reshape(dbsz, dseq, SGU_W))

    st = lambda name: jnp.stack(outs[name])
    return (x[:n_p].reshape(bsz, seq, d), x[n_p:n_p + n_s].reshape(dbsz, dseq, d), st("pool_p"), st("pool_s"),
            st("k_p"), st("v_p"), st("k_s"), st("v_s"), st("sgu_s"))
```

```python
import functools

import jax
import jax.numpy as jnp
from jax import lax
from jax.experimental import pallas as pl
from jax.experimental.pallas import tpu as pltpu

F32 = jnp.float32
BF16 = jnp.bfloat16

EPS = 1e-6
D_MODEL = 1024
POOL_W = 256
POOL_GW = 64
POOL_STATE = 15
SGU_W = 256
SGU_GW = 64
SGU_CHUNK = 128
ATT_HEADS = 8
HEAD_DIM = 64
ATT_W = ATT_HEADS * HEAD_DIM
CHUNK = 64
BAND = 512
REL_CLIP = 128
N_BRANCH = 3
PEER_HEADS = 8
N_KEYS = 128
PK_HALF = 128
PK_TOPK = 16
NEG_BIG = -1e30

LANES = 128
SUBLANES = 8
ROW_TILE = 256
ATT_QB = 256
HALO = 16
TOPK_TT = 128
PEER_SLAB_ROWS = SUBLANES
PEER_N_SLABS = N_KEYS // PEER_SLAB_ROWS
PEER_SLABS_PER_STEP = 2
PEER_EC = PEER_SLABS_PER_STEP * PEER_SLAB_ROWS * N_KEYS
PEER_NC = N_KEYS * N_KEYS // PEER_EC
PEER_TOKEN_UNROLL = 16
PEER_TB = 512
VMEM_LIMIT = 56 * 1024 * 1024
PEER_VMEM_LIMIT = 60 * 1024 * 1024


def _cparams(sem):
    return pltpu.CompilerParams(dimension_semantics=sem, vmem_limit_bytes=VMEM_LIMIT)


def _const_spec(shape):
    n = len(shape)
    return pl.BlockSpec(shape, lambda *_: (0,) * n)


def _nt_dot(a, b):
    return lax.dot_general(a, b, (((1,), (1,)), ((), ())), preferred_element_type=F32)


def _split_bf16(x):
    hi = x.astype(BF16)
    lo = (x - hi.astype(F32)).astype(BF16)
    return hi, lo


def _row_specs(xa, xb, xb_tile0, n_prompt_tiles, width):
    tm = ROW_TILE
    last_b = xb.shape[0] // tm - 1
    return [pl.BlockSpec((tm, width), lambda i: (jnp.minimum(i, n_prompt_tiles - 1), 0)),
            pl.BlockSpec((tm, width), lambda i: (jnp.clip(xb_tile0 + i - n_prompt_tiles, xb_tile0, last_b), 0))]


def _in_proj_kernel(xa_ref, xb_ref, gmix_ref, win_ref, sgn_ref, qn_ref, kn_ref, hsum_ref, ms_ref, bs_ref,
                    pw_ref, psc_ref, a_ref, ya_ref, yb_ref, vn_ref, q_ref, k_ref, v_ref, halo_ref,
                    *, n_prompt_tiles, tiles_per_stream):
    i = pl.program_id(0)
    x = jnp.where(i < n_prompt_tiles, xa_ref[...], xb_ref[...])
    xn = x * lax.rsqrt(jnp.mean(x * x, axis=-1, keepdims=True) + EPS) * gmix_ref[...]
    z = jnp.dot(xn.astype(BF16), win_ref[...], preferred_element_type=F32)
    a = z[:, 0:POOL_W]
    a_ref[...] = a
    @pl.when(i % tiles_per_stream == 0)
    def _():
        halo_ref[...] = jnp.zeros_like(halo_ref)

    ya_ref[...] = _pool_tile(a, halo_ref[...], pw_ref, psc_ref)
    halo_ref[...] = a[a.shape[0] - HALO:, :]
    u = z[:, POOL_W:POOL_W + SGU_W]
    vb = z[:, POOL_W + SGU_W:POOL_W + 2 * SGU_W]
    vn = vb * lax.rsqrt(jnp.mean(vb * vb, axis=-1, keepdims=True) + EPS) * sgn_ref[...]
    vn_ref[...] = vn
    vnb = vn.astype(BF16)
    lane_group = lax.broadcasted_iota(jnp.int32, (SGU_CHUNK, SGU_W), 1) // SGU_GW
    rows = x.shape[0]
    for c in range(rows // SGU_CHUNK):
        sl = slice(c * SGU_CHUNK, (c + 1) * SGU_CHUNK)
        vc = vnb[sl, :]
        mixed = jnp.dot(ms_ref[0, 0], vc, preferred_element_type=F32)
        for g in range(1, SGU_W // SGU_GW):
            mg = jnp.dot(ms_ref[0, g], vc, preferred_element_type=F32)
            mixed = jnp.where(lane_group == g, mg, mixed)
        yb_ref[sl, :] = (u[sl, :] * (mixed + bs_ref[0])).astype(BF16)

    base = POOL_W + 2 * SGU_W
    q = z[:, base:base + ATT_W]
    k = z[:, base + ATT_W:base + 2 * ATT_W]
    v_ref[...] = z[:, base + 2 * ATT_W:base + 3 * ATT_W]

    def head_norm(t, w_ref):
        m = jnp.dot((t * t).astype(BF16), hsum_ref[...], preferred_element_type=F32)
        return t * lax.rsqrt(m + EPS) * w_ref[...]

    q_ref[...] = (head_norm(q, qn_ref) * (HEAD_DIM ** -0.5)).astype(BF16)
    k_ref[...] = head_norm(k, kn_ref)


def _in_proj(xa, xb, xb_tile0, n, gmix, win, sgn, qn, kn, hsum, ms, bs, pw, psc, n_prompt_tiles,
             tiles_per_stream):
    tm = ROW_TILE
    in_w = win.shape[1]
    row = lambda w: pl.BlockSpec((tm, w), lambda i: (i, 0))
    sel = lambda i: (jnp.where(i >= n_prompt_tiles, 1, 0), 0, 0, 0)
    sel3 = lambda i: (jnp.where(i >= n_prompt_tiles, 1, 0), 0, 0)
    return pl.pallas_call(
        functools.partial(_in_proj_kernel, n_prompt_tiles=n_prompt_tiles, tiles_per_stream=tiles_per_stream),
        grid=(n // tm,),
        in_specs=_row_specs(xa, xb, xb_tile0, n_prompt_tiles, D_MODEL) + [
                  _const_spec((1, D_MODEL)), _const_spec((D_MODEL, in_w)),
                  _const_spec((1, SGU_W)), _const_spec((1, ATT_W)), _const_spec((1, ATT_W)),
                  _const_spec((ATT_W, ATT_W)),
                  pl.BlockSpec((1, SGU_W // SGU_GW, SGU_CHUNK, SGU_CHUNK), sel),
                  pl.BlockSpec((1, SGU_CHUNK, SGU_W), sel3),
                  _const_spec((POOL_W, POOL_W)), _const_spec((1, POOL_W))],
        out_specs=[row(POOL_W), row(POOL_W), row(SGU_W), row(SGU_W), row(ATT_W), row(ATT_W), row(ATT_W)],
        out_shape=[jax.ShapeDtypeStruct((n, POOL_W), F32), jax.ShapeDtypeStruct((n, POOL_W), BF16),
                   jax.ShapeDtypeStruct((n, SGU_W), BF16),
                   jax.ShapeDtypeStruct((n, SGU_W), F32), jax.ShapeDtypeStruct((n, ATT_W), BF16),
                   jax.ShapeDtypeStruct((n, ATT_W), F32), jax.ShapeDtypeStruct((n, ATT_W), F32)],
        scratch_shapes=[pltpu.VMEM((HALO, POOL_W), F32)],
        compiler_params=_cparams(("arbitrary",)),
        name="in_proj",
    )(xa, xb, gmix, win, sgn, qn, kn, hsum, ms, bs, pw, psc)


def _pool_tile(a, halo, pw_ref, sc_ref):
    tm = a.shape[0]
    e = jnp.concatenate([halo, a], axis=0)
    s2 = e[1:] + e[:-1]
    s4 = s2[2:] + s2[:-2]
    s8 = s4[4:] + s4[:-4]
    s16 = s8[8:] + s8[:-8]
    lg = lax.broadcasted_iota(jnp.int32, (tm, POOL_W), 1) // POOL_GW
    mean = jnp.where(lg == 0, s2[15:] * 0.5,
                     jnp.where(lg == 1, s4[13:] * 0.25,
                               jnp.where(lg == 2, s8[9:] * 0.125, s16[1:] * 0.0625)))
    pooled = mean - a
    y = jnp.dot(pooled.astype(BF16), pw_ref[...], preferred_element_type=F32) * sc_ref[...]
    return y.astype(BF16)


def _pool_kernel(a_ref, prev_ref, pre_ref, pw_ref, sc_ref, y_ref):
    halo = jnp.where(pl.program_id(1) == 0, pre_ref[0], prev_ref[...])
    y_ref[...] = _pool_tile(a_ref[...], halo, pw_ref, sc_ref)


def _pool(a, prefix, pw, sc, row0, n_streams, t_len, tm):
    nt = t_len // tm
    b0 = row0 // tm
    hb = tm // HALO
    return pl.pallas_call(
        _pool_kernel,
        grid=(n_streams, nt),
        in_specs=[pl.BlockSpec((tm, POOL_W), lambda b, t: (b0 + b * nt + t, 0)),
                  pl.BlockSpec((HALO, POOL_W), lambda b, t: (jnp.maximum((b0 + b * nt + t) * hb - 1, 0), 0)),
                  pl.BlockSpec((1, HALO, POOL_W), lambda b, t: (b, 0, 0)),
                  _const_spec((POOL_W, POOL_W)), _const_spec((1, POOL_W))],
        out_specs=pl.BlockSpec((tm, POOL_W), lambda b, t: (b * nt + t, 0)),
        out_shape=jax.ShapeDtypeStruct((n_streams * t_len, POOL_W), BF16),
        compiler_params=_cparams(("arbitrary", "arbitrary")),
        name="pool_mix",
    )(a, a, prefix, pw, sc)


def _attend(q, k, v, bias_ref, col_bias, o_ref):
    qb = q.shape[0]
    lane_hi = lax.broadcasted_iota(jnp.int32, (qb, 2 * HEAD_DIM), 1) >= HEAD_DIM
    for hp in range(ATT_HEADS // 2):
        sl = slice(hp * 2 * HEAD_DIM, (hp + 1) * 2 * HEAD_DIM)
        q2, k2, v2 = q[:, sl], k[:, sl], v[:, sl]
        out = None
        for sub in range(2):
            qm = jnp.where(lane_hi == (sub == 1), q2, jnp.zeros_like(q2))
            s = _nt_dot(qm, k2) + bias_ref[hp * 2 + sub]
            if col_bias is not None:
                s = s + col_bias
            m = jnp.max(s, axis=-1, keepdims=True)
            p = jnp.exp(s - m)
            l = jnp.sum(p, axis=-1, keepdims=True)
            o = jnp.dot(p.astype(BF16), v2, preferred_element_type=F32) / l
            out = o if sub == 0 else jnp.where(lane_hi, o, out)
        o_ref[:, sl] = out.astype(BF16)


def _attn_prompt_kernel(q_ref, k0_ref, k1_ref, k2_ref, v0_ref, v1_ref, v2_ref, bias_ref, o_ref):
    t = pl.program_id(1)
    qb = q_ref.shape[0]
    k = jnp.concatenate([k0_ref[...], k1_ref[...], k2_ref[...]], axis=0).astype(BF16)
    v = jnp.concatenate([v0_ref[...], v1_ref[...], v2_ref[...]], axis=0).astype(BF16)
    @pl.when(t < 2)
    def _():
        key_row = (t - 2) * qb + lax.broadcasted_iota(jnp.int32, (1, 3 * qb), 1)
        col_bias = jnp.where(key_row >= 0, 0.0, NEG_BIG).astype(F32)
        _attend(q_ref[...], k, v, bias_ref, col_bias, o_ref)

    @pl.when(t >= 2)
    def _():
        _attend(q_ref[...], k, v, bias_ref, None, o_ref)


def _attn_prompt(q, k, v, bias, n_streams, t_len):
    qb = ATT_QB
    nt = t_len // qb
    cur = lambda b, t: (b * nt + t, 0)
    prev1 = lambda b, t: (b * nt + jnp.maximum(t - 1, 0), 0)
    prev2 = lambda b, t: (b * nt + jnp.maximum(t - 2, 0), 0)
    blk = lambda im: pl.BlockSpec((qb, ATT_W), im)
    return pl.pallas_call(
        _attn_prompt_kernel,
        grid=(n_streams, nt),
        in_specs=[blk(cur), blk(prev2), blk(prev1), blk(cur), blk(prev2), blk(prev1), blk(cur),
                  _const_spec(bias.shape)],
        out_specs=blk(cur),
        out_shape=jax.ShapeDtypeStruct((n_streams * t_len, ATT_W), BF16),
        compiler_params=_cparams(("arbitrary", "arbitrary")),
        name="attn_prompt",
    )(q, k, k, k, v, v, v, bias)


def _attn_sample_kernel(q_ref, kn_ref, vn_ref, kc_ref, vc_ref, bias_ref, o_ref):
    k = jnp.concatenate([kc_ref[0], kn_ref[...]], axis=0).astype(BF16)
    v = jnp.concatenate([vc_ref[0], vn_ref[...]], axis=0).astype(BF16)
    _attend(q_ref[...], k, v, bias_ref, None, o_ref)


def _attn_sample(q, k, v, kc, vc, bias, row0, n_streams, s_len):
    b0 = row0 // s_len
    n_cache = kc.shape[1]
    new = pl.BlockSpec((s_len, ATT_W), lambda s: (b0 + s, 0))
    cache = pl.BlockSpec((1, n_cache, ATT_W), lambda s: (s, 0, 0))
    return pl.pallas_call(
        _attn_sample_kernel,
        grid=(n_streams,),
        in_specs=[new, new, new, cache, cache, _const_spec(bias.shape)],
        out_specs=pl.BlockSpec((s_len, ATT_W), lambda s: (s, 0)),
        out_shape=jax.ShapeDtypeStruct((n_streams * s_len, ATT_W), BF16),
        compiler_params=_cparams(("arbitrary",)),
        name="attn_sample",
    )(q, k, v, kc, vc, bias)


def _merge_kernel(xa_ref, xb_ref, yap_ref, yas_ref, yb_ref, ycp_ref, ycs_ref, gmix_ref, wg_ref, bg_ref, wa_ref,
                  wb_ref, wc_ref, wo_ref, gffn_ref, wq_ref, sk_ref, h_ref, hn_ref, st_ref, *, n_prompt_tiles):
    is_prompt = pl.program_id(0) < n_prompt_tiles
    x = jnp.where(is_prompt, xa_ref[...], xb_ref[...])
    xn = (x * lax.rsqrt(jnp.mean(x * x, axis=-1, keepdims=True) + EPS) * gmix_ref[...]).astype(BF16)
    ya = jnp.where(is_prompt, yap_ref[...], yas_ref[...])
    yc = jnp.where(is_prompt, ycp_ref[...], ycs_ref[...])
    merged = None
    for b, (y, w_ref) in enumerate(((ya, wa_ref), (yb_ref[...], wb_ref), (yc, wc_ref))):
        cols = slice(b * D_MODEL, (b + 1) * D_MODEL)
        gate = jax.nn.sigmoid(jnp.dot(xn, wg_ref[:, cols], preferred_element_type=F32) + bg_ref[:, cols])
        term = gate * jnp.dot(y, w_ref[...], preferred_element_type=F32)
        merged = term if merged is None else merged + term
    h = x + jnp.dot(merged.astype(BF16), wo_ref[...], preferred_element_type=F32)
    h_ref[...] = h
    hn = (h * lax.rsqrt(jnp.mean(h * h, axis=-1, keepdims=True) + EPS) * gffn_ref[...]).astype(BF16)
    hn_ref[...] = hn
    qp = jnp.dot(hn, wq_ref[...], preferred_element_type=F32).astype(BF16)
    for hp in range(2 * PEER_HEADS):
        st_ref[hp] = _nt_dot(sk_ref[hp % 2], qp[:, hp * PK_HALF:(hp + 1) * PK_HALF])


def _merge(xa, xb, xb_tile0, n, npt, ya_p, ya_s, yb, yc_p, yc_s, gmix, wg, bg, wa, wb, wc, wo, gffn, wq, sk):
    tm = ROW_TILE
    nst = ya_s.shape[0] // tm
    row = lambda w: pl.BlockSpec((tm, w), lambda i: (i, 0))
    prow = lambda w: pl.BlockSpec((tm, w), lambda i: (jnp.minimum(i, npt - 1), 0))
    srow = lambda w: pl.BlockSpec((tm, w), lambda i: (jnp.clip(i - npt, 0, nst - 1), 0))
    return pl.pallas_call(
        functools.partial(_merge_kernel, n_prompt_tiles=npt),
        grid=(n // tm,),
        in_specs=_row_specs(xa, xb, xb_tile0, npt, D_MODEL) + [
                  prow(POOL_W), srow(POOL_W), row(SGU_W), prow(ATT_W), srow(ATT_W),
                  _const_spec((1, D_MODEL)), _const_spec(wg.shape), _const_spec(bg.shape),
                  _const_spec(wa.shape), _const_spec(wb.shape), _const_spec(wc.shape),
                  _const_spec(wo.shape), _const_spec((1, D_MODEL)), _const_spec(wq.shape),
                  _const_spec(sk.shape)],
        out_specs=[row(D_MODEL), row(D_MODEL),
                   pl.BlockSpec((2 * PEER_HEADS, N_KEYS, tm), lambda i: (0, 0, i))],
        out_shape=[jax.ShapeDtypeStruct((n, D_MODEL), F32), jax.ShapeDtypeStruct((n, D_MODEL), BF16),
                   jax.ShapeDtypeStruct((2 * PEER_HEADS, N_KEYS, n), F32)],
        compiler_params=_cparams(("arbitrary",)),
        name="merge",
    )(xa, xb, ya_p, ya_s, yb, yc_p, yc_s, gmix, wg, bg, wa, wb, wc, wo, gffn, wq, sk)


def _top16(x, iota, payloads=()):
    n = x.shape[0]
    vals, idxs = [], []
    picked = [[] for _ in payloads]
    for _ in range(PK_TOPK):
        m = jnp.max(x, axis=0, keepdims=True)
        idx = jnp.min(jnp.where(x == m, iota, float(n)), axis=0, keepdims=True)
        hit = iota == idx
        vals.append(m)
        idxs.append(idx)
        for lst, p in zip(picked, payloads):
            lst.append(jnp.sum(jnp.where(hit, p, 0.0), axis=0, keepdims=True))
        x = jnp.where(hit, -jnp.inf, x)
    return vals, idxs, picked


_KEY_LISTS = 4


def _top16_keys(x):
    n, tt = x.shape
    rows = n // _KEY_LISTS
    base = lax.broadcasted_iota(jnp.int32, (rows, tt), 0).astype(F32)
    val = [x[l * rows:(l + 1) * rows] for l in range(_KEY_LISTS)]
    idx = [base + float(l * rows) for l in range(_KEY_LISTS)]
    for a in [p for end in range(_KEY_LISTS - 1, 0, -1) for p in range(end)]:
        swap = val[a] < val[a + 1]
        val[a], val[a + 1] = jnp.where(swap, val[a + 1], val[a]), jnp.where(swap, val[a], val[a + 1])
        idx[a], idx[a + 1] = jnp.where(swap, idx[a + 1], idx[a]), jnp.where(swap, idx[a], idx[a + 1])
    vals, idxs = [], []
    for _ in range(PK_TOPK):
        m = jnp.max(val[0], axis=0, keepdims=True)
        sel = jnp.min(jnp.where(val[0] == m, idx[0], float(n)), axis=0, keepdims=True)
        hit = idx[0] == sel
        vals.append(m)
        idxs.append(sel)
        for l in range(_KEY_LISTS - 1):
            val[l] = jnp.where(hit, val[l + 1], val[l])
            idx[l] = jnp.where(hit, idx[l + 1], idx[l])
        val[-1] = jnp.where(hit, -jnp.inf, val[-1])
    return vals, idxs


_PAIRS = [(a, b) for a in range(PK_TOPK) for b in range(PK_TOPK) if (a + 1) * (b + 1) <= PK_TOPK]
_PAIR_ROWS = -(-len(_PAIRS) // SUBLANES) * SUBLANES


def _select_head(s0, s1):
    tt = s0.shape[1]
    iota_c = lax.broadcasted_iota(jnp.int32, (_PAIR_ROWS, tt), 0).astype(F32)
    pad = _PAIR_ROWS - len(_PAIRS)
    neg_row = jnp.full((1, tt), -jnp.inf, F32)
    zero_row = jnp.zeros((1, tt), F32)
    v0, i0 = _top16_keys(s0)
    v1, i1 = _top16_keys(s1)
    cand = jnp.concatenate([v0[a] + v1[b] for a, b in _PAIRS] + [neg_row] * pad, axis=0)
    cand_i = jnp.concatenate([i0[a] for a, _ in _PAIRS] + [zero_row] * pad, axis=0)
    cand_j = jnp.concatenate([i1[b] for _, b in _PAIRS] + [zero_row] * pad, axis=0)
    tv, _, (ti, tj) = _top16(cand, iota_c, (cand_i, cand_j))
    e = jnp.exp(jnp.concatenate(tv, axis=0) - tv[0])
    gates = e / jnp.sum(e, axis=0, keepdims=True)
    return jnp.concatenate(ti, axis=0), jnp.concatenate(tj, axis=0), gates


PEER_STEPS_PER_TILE = 2 * PEER_NC // (PEER_TB // TOPK_TT)
PEER_HEADS_PER_STEP = PEER_HEADS // PEER_STEPS_PER_TILE
PEER_NSEL = PEER_HEADS * PK_TOPK


def _peer_kernel(hn_ref, h_ref, st_ref, uv_ref, o_ref, hs_ref, gm_ref, sel_ref, stage_ref):
    b = pl.program_id(0)
    s = pl.program_id(1)
    tb = hn_ref.shape[0]
    rpc = PEER_SLAB_ROWS
    sps = PEER_SLABS_PER_STEP
    slot_new = b % 2
    slot_cur = 1 - slot_new
    first = b == 0
    last = b == pl.num_programs(0) - 1
    middle = jnp.logical_not(jnp.logical_or(first, last))

    def select_step():
        part = s % PEER_STEPS_PER_TILE
        rows = None
        for k in range(PEER_HEADS_PER_STEP):
            s0, s1 = st_ref[2 * k], st_ref[2 * k + 1]
            if rows is not None:
                anchor = 0.0 * rows[2][0:1, :]
                s0, s1 = s0 + anchor, s1 + anchor
            rows = _select_head(s0, s1)
            r0 = pl.multiple_of(part * (PEER_HEADS_PER_STEP * PK_TOPK), PEER_HEADS_PER_STEP * PK_TOPK)
            for x, val in enumerate(rows):
                stage_ref[x, pl.ds(r0 + k * PK_TOPK, PK_TOPK), :] = val

    def scores_pass():
        hc = _nt_dot(hn_ref[...], uv_ref[...])
        for k in range(sps):
            slab = hs_ref.at[s * sps + k]
            for ib in range(rpc):
                col = (k * rpc + ib) * N_KEYS
                slab[pl.ds(ib, tb, stride=rpc), :] = hc[:, col:col + N_KEYS]

    def combine_pass():
        c = s - PEER_NC
        wc = jnp.concatenate([hs_ref.at[c * sps + k][pl.ds(ib, tb, stride=rpc), :]
                              for k in range(sps) for ib in range(rpc)], axis=1)
        o_ref[...] += jnp.dot(wc.astype(BF16), uv_ref[...], preferred_element_type=F32)

    @pl.when(first)
    def _():
        select_step()

    @pl.when(jnp.logical_and(s < PEER_NC, middle))
    def _():
        select_step()
        scores_pass()

    @pl.when(jnp.logical_and(s < PEER_NC, last))
    def _():
        scores_pass()

    @pl.when(jnp.logical_and(s == PEER_NC - 1, jnp.logical_not(first)))
    def _weights():
        sub = lax.broadcasted_iota(jnp.int32, (N_KEYS, N_KEYS), 0).astype(F32).astype(BF16)
        one = jnp.ones((N_KEYS, N_KEYS), BF16)
        zero = jnp.zeros((N_KEYS, N_KEYS), BF16)

        def gate_matrix(n):
            irow = sel_ref[slot_cur, 0, pl.ds(n, 1), :].astype(BF16)
            jrow = sel_ref[slot_cur, 1, pl.ds(n, 1), :].astype(BF16)
            g_hi, g_lo = _split_bf16(0.5 * sel_ref[slot_cur, 2, pl.ds(n, 1), :])
            at = jnp.where(sub == irow, one, zero)
            jhit = sub == jrow
            b_hi = jnp.where(jhit, jnp.broadcast_to(g_hi, jhit.shape), zero)
            b_lo = jnp.where(jhit, jnp.broadcast_to(g_lo, jhit.shape), zero)
            return _nt_dot(jnp.concatenate([at, at], axis=1),
                           jnp.concatenate([b_hi, b_lo], axis=1))

        unroll = PEER_TOKEN_UNROLL
        n_groups = tb // unroll

        def make_gates(group, slot):
            group = jnp.minimum(group, n_groups - 1)
            for t in range(unroll):
                gm_ref[slot, t * N_KEYS:(t + 1) * N_KEYS, :] = gate_matrix(group * unroll + t)

        def apply_gates(group, slot):
            base = pl.multiple_of(group * (unroll * rpc), unroll * rpc)
            for c in range(PEER_N_SLABS):
                hv = hs_ref[c, pl.ds(base, unroll * rpc), :]
                gm = jnp.concatenate([gm_ref[slot, t * N_KEYS + c * rpc:t * N_KEYS + (c + 1) * rpc, :]
                                      for t in range(unroll)], axis=0)
                hs_ref[c, pl.ds(base, unroll * rpc), :] = (hv * gm) * (1.0 + lax.erf(hv * (2.0 ** -0.5)))

        def body(m, carry):
            apply_gates(2 * m, 0)
            make_gates(2 * m + 1, 1)
            apply_gates(2 * m + 1, 1)
            make_gates(2 * m + 2, 0)
            return carry

        o_ref[...] = h_ref[...]
        make_gates(0, 0)
        lax.fori_loop(0, n_groups // 2, body, 0)

    @pl.when(jnp.logical_and(s >= PEER_NC, middle))
    def _():
        select_step()
        combine_pass()

    @pl.when(jnp.logical_and(s >= PEER_NC, last))
    def _():
        combine_pass()

    @pl.when(jnp.logical_and(s % PEER_STEPS_PER_TILE == PEER_STEPS_PER_TILE - 1, jnp.logical_not(last)))
    def _publish():
        t0 = pl.multiple_of((s // PEER_STEPS_PER_TILE) * TOPK_TT, TOPK_TT)
        for x in range(3):
            sel_ref[slot_new, x, pl.ds(t0, TOPK_TT), :] = stage_ref[x].T


def _cast_tables_kernel(u_ref, v_ref, o_ref):
    o_ref[0] = u_ref[...].astype(BF16)
    o_ref[1] = v_ref[...].astype(BF16)


def _cast_tables(u, v):
    depth, e, d = u.shape
    rc = PEER_EC // 2
    src = pl.BlockSpec((None, rc, d), lambda l, r: (l, r, 0))
    return pl.pallas_call(
        _cast_tables_kernel,
        grid=(depth, e // rc),
        in_specs=[src, src],
        out_specs=pl.BlockSpec((None, 2, rc, d), lambda l, r: (l, 0, r, 0)),
        out_shape=jax.ShapeDtypeStruct((depth, 2, e, d), BF16),
        compiler_params=_cparams(("arbitrary", "arbitrary")),
        name="cast_tables",
    )(u, v)


def _peer(hn, h, st, uv, layer):
    n = hn.shape[0]
    tb = PEER_TB
    nb = n // tb
    nc = PEER_NC
    tiles = tb // TOPK_TT
    row = lambda w: pl.BlockSpec((tb, w), lambda b, s: (jnp.maximum(b - 1, 0), 0))
    st_rows = 2 * PEER_HEADS_PER_STEP
    st_spec = pl.BlockSpec(
        (st_rows, N_KEYS, TOPK_TT),
        lambda b, s: (s % PEER_STEPS_PER_TILE, 0, jnp.minimum(b, nb - 1) * tiles + s // PEER_STEPS_PER_TILE))
    return pl.pallas_call(
        _peer_kernel,
        grid=(nb + 1, 2 * nc),
        in_specs=[row(D_MODEL), row(D_MODEL), st_spec,
                  pl.BlockSpec((None, None, PEER_EC, D_MODEL), lambda b, s: (layer, s // nc, s % nc, 0))],
        out_specs=row(D_MODEL),
        out_shape=jax.ShapeDtypeStruct((n, D_MODEL), F32),
        scratch_shapes=[pltpu.VMEM((PEER_N_SLABS, tb * PEER_SLAB_ROWS, N_KEYS), F32),
                        pltpu.VMEM((2, PEER_TOKEN_UNROLL * N_KEYS, N_KEYS), F32),
                        pltpu.VMEM((2, 3, tb, PEER_NSEL), F32),
                        pltpu.VMEM((3, PEER_NSEL, TOPK_TT), F32)],
        compiler_params=pltpu.CompilerParams(dimension_semantics=("arbitrary", "arbitrary"),
                                             vmem_limit_bytes=PEER_VMEM_LIMIT),
        name="peer_experts",
    )(hn, h, st, uv)


def _bias_kernel(g_ref, full_ref, band_ref):
    _, nq, nk = full_ref.shape
    w = g_ref.shape[1]
    qc = lax.broadcasted_iota(jnp.int32, (nq, nk), 0) // CHUNK
    kc = lax.broadcasted_iota(jnp.int32, (nq, nk), 1) // CHUNK
    band = jnp.abs(2 * (kc - qc) - BAND // CHUNK) <= BAND // CHUNK
    for h in range(ATT_HEADS):
        line = jnp.broadcast_to(g_ref[h:h + 1, :], (nq, w))
        t = pltpu.roll(line, w - nq + 1, 1, stride=1, stride_axis=0)[:, :nk]
        full_ref[h] = t
        band_ref[h] = jnp.where(band, t, NEG_BIG)


def _rel_bias(table, nq, nk):
    w = 1024
    assert nq + nk - 1 <= w
    dist = nk - 1 - jnp.arange(w)
    line = table[:, jnp.clip(dist, -REL_CLIP, REL_CLIP) + REL_CLIP].astype(F32)
    shape = jax.ShapeDtypeStruct((ATT_HEADS, nq, nk), F32)
    return pl.pallas_call(
        _bias_kernel,
        out_shape=[shape, shape],
        compiler_params=pltpu.CompilerParams(vmem_limit_bytes=VMEM_LIMIT),
        name="rel_bias",
    )(line)


def _block_diag(blocks):
    g, r, c = blocks.shape
    eye = jnp.eye(g, dtype=blocks.dtype)
    return (eye[:, None, :, None] * blocks[:, :, None, :]).reshape(g * r, g * c)


def _sgu_mats(w_s, b_s, length):
    g = w_s.shape[0]
    reps = SGU_CHUNK // length
    tri = jnp.tril(jnp.ones((length, length), bool))
    ws = jnp.where(tri[None], w_s[:, :length, :length], 0.0)
    eye = jnp.eye(reps, dtype=ws.dtype)
    mats = (eye[None, :, None, :, None] * ws[:, None, :, None, :]).reshape(g, SGU_CHUNK, SGU_CHUNK)
    bias = jnp.tile(b_s[:, :length], (1, reps))
    bias = jnp.repeat(bias.T, SGU_GW, axis=1)
    return mats, bias


def kernel(x_prompt, x_sample, state_pool, cache_k, cache_v, g_mix, w_in, pool_w, pool_scale, sgu_norm, sgu_w, sgu_b, q_norm, k_norm, rel_table, w_gate, b_gate, w_br_a, w_br_b, w_br_c, w_out, g_ffn, peer_wq, peer_subkeys, peer_u, peer_v):
    depth = g_mix.shape[0]
    bsz, seq, d = x_prompt.shape
    dbsz, dseq, _ = x_sample.shape
    n_p, n_s = bsz * seq, dbsz * dseq
    n_cache = cache_k.shape[2]
    assert seq % ATT_QB == 0 and n_p % ROW_TILE == 0 and n_s % ROW_TILE == 0
    assert dseq <= SGU_CHUNK and SGU_CHUNK % dseq == 0 and dseq >= POOL_STATE + 1 and dseq % SUBLANES == 0
    assert n_cache == 2 * ATT_QB and dseq <= ATT_QB
    n_tot = -(-(n_p + n_s) // PEER_TB) * PEER_TB
    n_prompt_tiles = n_p // ROW_TILE
    xa, xb, xb_tile0 = x_prompt.reshape(n_p, d), x_sample.reshape(n_s, d), 0

    hsum = _block_diag(jnp.full((ATT_HEADS, HEAD_DIM, HEAD_DIM), 1.0 / HEAD_DIM, F32)).astype(BF16)
    uv_tables = _cast_tables(peer_u, peer_v)
    keep = min(BAND, seq)
    outs = {name: [] for name in ("pool_p", "pool_s", "k_p", "v_p", "k_s", "v_s", "sgu_s")}

    for l in range(depth):
        row = lambda a: a[l].reshape(1, -1)
        m_p, b_p = _sgu_mats(sgu_w[l], sgu_b[l], SGU_CHUNK)
        m_s, b_s = _sgu_mats(sgu_w[l], sgu_b[l], dseq)
        ms = jnp.stack([m_p, m_s]).astype(BF16)
        bs = jnp.stack([b_p, b_s])
        pw = _block_diag(pool_w[l]).astype(BF16)
        a, ya_p, yb, vn, q, k, v = _in_proj(
            xa, xb, xb_tile0, n_tot, row(g_mix), w_in[l].astype(BF16), row(sgu_norm),
            jnp.tile(q_norm[l], ATT_HEADS).reshape(1, -1), jnp.tile(k_norm[l], ATT_HEADS).reshape(1, -1),
            hsum, ms, bs, pw, row(pool_scale), n_prompt_tiles, seq // ROW_TILE)

        sample_prefix = jnp.pad(state_pool[l], ((0, 0), (HALO - POOL_STATE, 0), (0, 0)))
        ya_s = _pool(a, sample_prefix, pw, row(pool_scale), n_p, dbsz, dseq, dseq)

        bias_full, bias_band = _rel_bias(rel_table[l], ATT_QB, 3 * ATT_QB)
        yc_p = _attn_prompt(q, k, v, bias_band, bsz, seq)
        yc_s = _attn_sample(q, k, v, cache_k[l].reshape(dbsz, n_cache, ATT_W),
                            cache_v[l].reshape(dbsz, n_cache, ATT_W),
                            bias_full[:, :dseq, :n_cache + dseq], n_p, dbsz, dseq)

        h, hn, st = _merge(
            xa, xb, xb_tile0, n_tot, n_prompt_tiles, ya_p, ya_s, yb, yc_p, yc_s,
            row(g_mix), w_gate[l].astype(BF16), row(b_gate), w_br_a[l].astype(BF16),
            w_br_b[l].astype(BF16), w_br_c[l].astype(BF16), w_out[l].astype(BF16), row(g_ffn),
            peer_wq[l].astype(BF16), peer_subkeys[l].astype(BF16))
        x = _peer(hn, h, st, uv_tables, l)
        xa, xb, xb_tile0 = x, x, n_prompt_tiles

        def tail(arr, count):
            return jnp.stack([arr[(b + 1) * seq - count:(b + 1) * seq] for b in range(bsz)])

        sample = lambda arr: arr[n_p:n_p + n_s]
        outs["pool_p"].append(tail(a, POOL_STATE))
        outs["pool_s"].append(sample(a).reshape(dbsz, dseq, POOL_W)[:, dseq - POOL_STATE:])
        outs["k_p"].append(tail(k, keep).reshape(bsz, keep, ATT_HEADS, HEAD_DIM))
        outs["v_p"].append(tail(v, keep).reshape(bsz, keep, ATT_HEADS, HEAD_DIM))
        outs["k_s"].append(sample(k).reshape(dbsz, dseq, ATT_HEADS, HEAD_DIM))
        outs["v_s"].append(sample(v).reshape(dbsz, dseq, ATT_HEADS, HEAD_DIM))
        outs["sgu_s"].append(sample(vn).reshape(dbsz, dseq, SGU_W))

    st = lambda name: jnp.stack(outs[name])
    return (x[:n_p].reshape(bsz, seq, d), x[n_p:n_p + n_s].reshape(dbsz, dseq, d), st("pool_p"), st("pool_s"),
            st("k_p"), st("v_p"), st("k_s"), st("v_s"), st("sgu_s"))
```

```python
import functools

import jax
import jax.numpy as jnp
from jax import lax
from jax.experimental import pallas as pl
from jax.experimental.pallas import tpu as pltpu

F32 = jnp.float32
BF16 = jnp.bfloat16

EPS = 1e-6
D_MODEL = 1024
POOL_W = 256
POOL_GW = 64
POOL_STATE = 15
SGU_W = 256
SGU_GW = 64
SGU_CHUNK = 128
ATT_HEADS = 8
HEAD_DIM = 64
ATT_W = ATT_HEADS * HEAD_DIM
CHUNK = 64
BAND = 512
REL_CLIP = 128
PEER_HEADS = 8
N_KEYS = 128
PK_HALF = 128
PK_TOPK = 16
NEG_BIG = -1e30

LANES = 128
SUBLANES = 8
ROW_TILE = 256
ATT_QB = 256
HALO = 16
POOL_TILE = 1024
BIAS_LINE_W = 1024
TOPK_TT = LANES
PEER_SLAB_ROWS = SUBLANES
PEER_N_SLABS = N_KEYS // PEER_SLAB_ROWS
PEER_SLABS_PER_STEP = 4
PEER_SLABS_PER_DOT = 2
PEER_EC = PEER_SLABS_PER_STEP * PEER_SLAB_ROWS * N_KEYS
PEER_NC = N_KEYS * N_KEYS // PEER_EC
PEER_TOKEN_UNROLL = 32
PEER_TB = 384
VMEM_LIMIT = 56 * 1024 * 1024
PEER_VMEM_LIMIT = 60 * 1024 * 1024


def _cparams(sem):
    return pltpu.CompilerParams(dimension_semantics=sem, vmem_limit_bytes=VMEM_LIMIT)


def _const_spec(shape):
    n = len(shape)
    return pl.BlockSpec(shape, lambda *_: (0,) * n)


def _nt_dot(a, b):
    return lax.dot_general(a, b, (((1,), (1,)), ((), ())), preferred_element_type=F32)


def _split_bf16(x):
    hi = x.astype(BF16)
    lo = (x - hi.astype(F32)).astype(BF16)
    return hi, lo


def _row_specs(xa, xb, xb_tile0, n_prompt_tiles, width):
    tm = ROW_TILE
    last_b = xb.shape[0] // tm - 1
    return [pl.BlockSpec((tm, width), lambda i: (jnp.minimum(i, n_prompt_tiles - 1), 0)),
            pl.BlockSpec((tm, width), lambda i: (jnp.clip(xb_tile0 + i - n_prompt_tiles, xb_tile0, last_b), 0))]


def _in_proj_kernel(xa_ref, xb_ref, gmix_ref, win_ref, sgn_ref, qn_ref, kn_ref, hsum_ref, ms_ref, bs_ref,
                    a_ref, yb_ref, vn_ref, q_ref, k_ref, v_ref, *, n_prompt_tiles):
    x = jnp.where(pl.program_id(0) < n_prompt_tiles, xa_ref[...], xb_ref[...])
    xn = x * lax.rsqrt(jnp.mean(x * x, axis=-1, keepdims=True) + EPS) * gmix_ref[...]
    xnb = xn.astype(BF16)
    base = POOL_W + 2 * SGU_W
    z_sgu = jnp.dot(xnb, win_ref[:, POOL_W:base], preferred_element_type=F32)
    z_qk = jnp.dot(xnb, win_ref[:, base:base + 2 * ATT_W], preferred_element_type=F32)
    a_ref[...] = jnp.dot(xnb, win_ref[:, 0:POOL_W], preferred_element_type=F32)
    v_ref[...] = jnp.dot(xnb, win_ref[:, base + 2 * ATT_W:base + 3 * ATT_W], preferred_element_type=F32)
    u = z_sgu[:, 0:SGU_W]
    vb = z_sgu[:, SGU_W:2 * SGU_W]
    vn = vb * lax.rsqrt(jnp.mean(vb * vb, axis=-1, keepdims=True) + EPS) * sgn_ref[...]
    vn_ref[...] = vn
    vnb = vn.astype(BF16)
    lane_group = lax.broadcasted_iota(jnp.int32, (SGU_CHUNK, SGU_W), 1) // SGU_GW
    rows = x.shape[0]
    for c in range(rows // SGU_CHUNK):
        sl = slice(c * SGU_CHUNK, (c + 1) * SGU_CHUNK)
        vc = vnb[sl, :]
        mixed = jnp.dot(ms_ref[0, 0], vc, preferred_element_type=F32)
        for g in range(1, SGU_W // SGU_GW):
            mg = jnp.dot(ms_ref[0, g], vc, preferred_element_type=F32)
            mixed = jnp.where(lane_group == g, mg, mixed)
        yb_ref[sl, :] = (u[sl, :] * (mixed + bs_ref[0])).astype(BF16)

    q = z_qk[:, 0:ATT_W]
    k = z_qk[:, ATT_W:2 * ATT_W]

    def head_norm(t, w_ref):
        m = jnp.dot((t * t).astype(BF16), hsum_ref[...], preferred_element_type=F32)
        return t * lax.rsqrt(m + EPS) * w_ref[...]

    q_ref[...] = (head_norm(q, qn_ref) * (HEAD_DIM ** -0.5)).astype(BF16)
    k_ref[...] = head_norm(k, kn_ref)


def _in_proj(xa, xb, xb_tile0, n, gmix, win, sgn, qn, kn, hsum, ms, bs, n_prompt_tiles):
    tm = ROW_TILE
    in_w = win.shape[1]
    row = lambda w: pl.BlockSpec((tm, w), lambda i: (i, 0))
    sel = lambda i: (jnp.where(i >= n_prompt_tiles, 1, 0), 0, 0, 0)
    sel3 = lambda i: (jnp.where(i >= n_prompt_tiles, 1, 0), 0, 0)
    return pl.pallas_call(
        functools.partial(_in_proj_kernel, n_prompt_tiles=n_prompt_tiles),
        grid=(n // tm,),
        in_specs=_row_specs(xa, xb, xb_tile0, n_prompt_tiles, D_MODEL) + [
                  _const_spec((1, D_MODEL)), _const_spec((D_MODEL, in_w)),
                  _const_spec((1, SGU_W)), _const_spec((1, ATT_W)), _const_spec((1, ATT_W)),
                  _const_spec((ATT_W, ATT_W)),
                  pl.BlockSpec((1, SGU_W // SGU_GW, SGU_CHUNK, SGU_CHUNK), sel),
                  pl.BlockSpec((1, SGU_CHUNK, SGU_W), sel3)],
        out_specs=[row(POOL_W), row(SGU_W), row(SGU_W), row(ATT_W), row(ATT_W), row(ATT_W)],
        out_shape=[jax.ShapeDtypeStruct((n, POOL_W), F32), jax.ShapeDtypeStruct((n, SGU_W), BF16),
                   jax.ShapeDtypeStruct((n, SGU_W), F32), jax.ShapeDtypeStruct((n, ATT_W), BF16),
                   jax.ShapeDtypeStruct((n, ATT_W), F32), jax.ShapeDtypeStruct((n, ATT_W), F32)],
        compiler_params=_cparams(("arbitrary",)),
        name="in_proj",
    )(xa, xb, gmix, win, sgn, qn, kn, hsum, ms, bs)


def _pool_tile(a, halo, pw_ref, sc_ref):
    tm = a.shape[0]
    e = jnp.concatenate([halo, a], axis=0)
    s2 = e[1:] + e[:-1]
    s4 = s2[2:] + s2[:-2]
    s8 = s4[4:] + s4[:-4]
    s16 = s8[8:] + s8[:-8]
    lg = lax.broadcasted_iota(jnp.int32, (tm, POOL_W), 1) // POOL_GW
    mean = jnp.where(lg == 0, s2[15:] * 0.5,
                     jnp.where(lg == 1, s4[13:] * 0.25,
                               jnp.where(lg == 2, s8[9:] * 0.125, s16[1:] * 0.0625)))
    pooled = mean - a
    y = jnp.dot(pooled.astype(BF16), pw_ref[...], preferred_element_type=F32) * sc_ref[...]
    return y.astype(BF16)


def _pool_kernel(a_ref, prev_ref, pre_ref, pw_ref, sc_ref, y_ref):
    halo = jnp.where(pl.program_id(1) == 0, pre_ref[0], prev_ref[...])
    y_ref[...] = _pool_tile(a_ref[...], halo, pw_ref, sc_ref)


def _pool(a, prefix, pw, sc, row0, n_streams, t_len, tm):
    nt = t_len // tm
    b0 = row0 // tm
    hb = tm // HALO
    return pl.pallas_call(
        _pool_kernel,
        grid=(n_streams, nt),
        in_specs=[pl.BlockSpec((tm, POOL_W), lambda b, t: (b0 + b * nt + t, 0)),
                  pl.BlockSpec((HALO, POOL_W), lambda b, t: (jnp.maximum((b0 + b * nt + t) * hb - 1, 0), 0)),
                  pl.BlockSpec((1, HALO, POOL_W), lambda b, t: (b, 0, 0)),
                  _const_spec((POOL_W, POOL_W)), _const_spec((1, POOL_W))],
        out_specs=pl.BlockSpec((tm, POOL_W), lambda b, t: (b * nt + t, 0)),
        out_shape=jax.ShapeDtypeStruct((n_streams * t_len, POOL_W), BF16),
        compiler_params=_cparams(("arbitrary", "arbitrary")),
        name="pool_mix",
    )(a, a, prefix, pw, sc)


def _attend(q, k, v, bias_ref, col_bias, o_ref, windows):
    def pair_slice(hp):
        return slice(hp * 2 * HEAD_DIM, (hp + 1) * 2 * HEAD_DIM)

    def pair_scores(hp, rows, cols, lane_hi):
        q2, k2 = q[rows, pair_slice(hp)], k[cols, pair_slice(hp)]
        scores = []
        for sub in range(2):
            qm = jnp.where(lane_hi == (sub == 1), q2, jnp.zeros_like(q2))
            s = _nt_dot(qm, k2) + bias_ref[hp * 2 + sub, rows, cols]
            if col_bias is not None:
                s = s + col_bias[:, cols]
            scores.append(s)
        return scores

    def pair_output(hp, scores, rows, cols, lane_hi):
        v2 = v[cols, pair_slice(hp)]
        out = None
        for sub, s in enumerate(scores):
            m = jnp.max(s, axis=-1, keepdims=True)
            p = jnp.exp(s - m)
            l = jnp.sum(p, axis=-1, keepdims=True)
            o = jnp.dot(p.astype(BF16), v2, preferred_element_type=F32) / l
            out = o if sub == 0 else jnp.where(lane_hi, o, out)
        o_ref[rows, pair_slice(hp)] = out.astype(BF16)

    for rows, cols in windows:
        lane_hi = lax.broadcasted_iota(jnp.int32, (rows.stop - rows.start, 2 * HEAD_DIM), 1) >= HEAD_DIM
        for hp in range(ATT_HEADS // 2):
            pair_output(hp, pair_scores(hp, rows, cols, lane_hi), rows, cols, lane_hi)


def _attn_prompt_kernel(q_ref, k0_ref, k1_ref, k2_ref, v0_ref, v1_ref, v2_ref, bias_ref, o_ref):
    t = pl.program_id(1)
    qb = q_ref.shape[0]
    k = jnp.concatenate([k0_ref[...], k1_ref[...], k2_ref[...]], axis=0).astype(BF16)
    v = jnp.concatenate([v0_ref[...], v1_ref[...], v2_ref[...]], axis=0).astype(BF16)
    key_row = (t - 2) * qb + lax.broadcasted_iota(jnp.int32, (1, 3 * qb), 1)
    col_bias = jnp.where(key_row >= 0, 0.0, NEG_BIG).astype(F32)
    half, band = qb // 2, BAND + qb // 2
    windows = [(slice(r0, r0 + half), slice(r0, r0 + band)) for r0 in (0, half)]
    _attend(q_ref[...], k, v, bias_ref, col_bias, o_ref, windows)


def _attn_prompt(q, k, v, bias, n_streams, t_len):
    qb = ATT_QB
    nt = t_len // qb
    cur = lambda b, t: (b * nt + t, 0)
    prev1 = lambda b, t: (b * nt + jnp.maximum(t - 1, 0), 0)
    prev2 = lambda b, t: (b * nt + jnp.maximum(t - 2, 0), 0)
    blk = lambda im: pl.BlockSpec((qb, ATT_W), im)
    return pl.pallas_call(
        _attn_prompt_kernel,
        grid=(n_streams, nt),
        in_specs=[blk(cur), blk(prev2), blk(prev1), blk(cur), blk(prev2), blk(prev1), blk(cur),
                  _const_spec(bias.shape)],
        out_specs=blk(cur),
        out_shape=jax.ShapeDtypeStruct((n_streams * t_len, ATT_W), BF16),
        compiler_params=_cparams(("arbitrary", "arbitrary")),
        name="attn_prompt",
    )(q, k, k, k, v, v, v, bias)


def _attn_sample_kernel(q_ref, kn_ref, vn_ref, kc_ref, vc_ref, bias_ref, o_ref):
    k = jnp.concatenate([kc_ref[0], kn_ref[...]], axis=0).astype(BF16)
    v = jnp.concatenate([vc_ref[0], vn_ref[...]], axis=0).astype(BF16)
    _attend(q_ref[...], k, v, bias_ref, None, o_ref, [(slice(0, q_ref.shape[0]), slice(0, k.shape[0]))])


def _attn_sample(q, k, v, kc, vc, bias, row0, n_streams, s_len):
    b0 = row0 // s_len
    n_cache = kc.shape[1]
    new = pl.BlockSpec((s_len, ATT_W), lambda s: (b0 + s, 0))
    cache = pl.BlockSpec((1, n_cache, ATT_W), lambda s: (s, 0, 0))
    return pl.pallas_call(
        _attn_sample_kernel,
        grid=(n_streams,),
        in_specs=[new, new, new, cache, cache, _const_spec(bias.shape)],
        out_specs=pl.BlockSpec((s_len, ATT_W), lambda s: (s, 0)),
        out_shape=jax.ShapeDtypeStruct((n_streams * s_len, ATT_W), BF16),
        compiler_params=_cparams(("arbitrary",)),
        name="attn_sample",
    )(q, k, v, kc, vc, bias)


def _merge_kernel(xa_ref, xb_ref, yap_ref, yas_ref, yb_ref, ycp_ref, ycs_ref, gmix_ref, wg_ref, bg_ref, wa_ref,
                  wb_ref, wc_ref, wo_ref, gffn_ref, wq_ref, sk_ref, h_ref, hn_ref, st_ref, *, n_prompt_tiles):
    is_prompt = pl.program_id(0) < n_prompt_tiles
    x = jnp.where(is_prompt, xa_ref[...], xb_ref[...])
    xn = (x * lax.rsqrt(jnp.mean(x * x, axis=-1, keepdims=True) + EPS) * gmix_ref[...]).astype(BF16)
    ya = jnp.where(is_prompt, yap_ref[...], yas_ref[...])
    yc = jnp.where(is_prompt, ycp_ref[...], ycs_ref[...])
    branches = ((ya, wa_ref), (yb_ref[...], wb_ref), (yc, wc_ref))
    pre = [jnp.dot(xn, wg_ref[:, b * D_MODEL:(b + 1) * D_MODEL], preferred_element_type=F32)
           for b in range(len(branches))]
    proj = [jnp.dot(y, w_ref[...], preferred_element_type=F32) for y, w_ref in branches]
    merged = None
    for b in range(len(branches)):
        gate = jax.nn.sigmoid(pre[b] + bg_ref[:, b * D_MODEL:(b + 1) * D_MODEL])
        term = gate * proj[b]
        merged = term if merged is None else merged + term
    h = x + jnp.dot(merged.astype(BF16), wo_ref[...], preferred_element_type=F32)
    h_ref[...] = h
    hn = (h * lax.rsqrt(jnp.mean(h * h, axis=-1, keepdims=True) + EPS) * gffn_ref[...]).astype(BF16)
    hn_ref[...] = hn
    qp = jnp.dot(hn, wq_ref[...], preferred_element_type=F32).astype(BF16)
    for hp in range(2 * PEER_HEADS):
        st_ref[hp] = _nt_dot(sk_ref[hp % 2], qp[:, hp * PK_HALF:(hp + 1) * PK_HALF])


def _merge(xa, xb, xb_tile0, n, npt, ya_p, ya_s, yb, yc_p, yc_s, gmix, wg, bg, wa, wb, wc, wo, gffn, wq, sk):
    tm = ROW_TILE
    nst = ya_s.shape[0] // tm
    row = lambda w: pl.BlockSpec((tm, w), lambda i: (i, 0))
    prow = lambda w: pl.BlockSpec((tm, w), lambda i: (jnp.minimum(i, npt - 1), 0))
    srow = lambda w: pl.BlockSpec((tm, w), lambda i: (jnp.clip(i - npt, 0, nst - 1), 0))
    return pl.pallas_call(
        functools.partial(_merge_kernel, n_prompt_tiles=npt),
        grid=(n // tm,),
        in_specs=_row_specs(xa, xb, xb_tile0, npt, D_MODEL) + [
                  prow(POOL_W), srow(POOL_W), row(SGU_W), prow(ATT_W), srow(ATT_W),
                  _const_spec((1, D_MODEL)), _const_spec(wg.shape), _const_spec(bg.shape),
                  _const_spec(wa.shape), _const_spec(wb.shape), _const_spec(wc.shape),
                  _const_spec(wo.shape), _const_spec((1, D_MODEL)), _const_spec(wq.shape),
                  _const_spec(sk.shape)],
        out_specs=[row(D_MODEL), row(D_MODEL),
                   pl.BlockSpec((2 * PEER_HEADS, N_KEYS, tm), lambda i: (0, 0, i))],
        out_shape=[jax.ShapeDtypeStruct((n, D_MODEL), F32), jax.ShapeDtypeStruct((n, D_MODEL), BF16),
                   jax.ShapeDtypeStruct((2 * PEER_HEADS, N_KEYS, n), F32)],
        compiler_params=_cparams(("arbitrary",)),
        name="merge",
    )(xa, xb, ya_p, ya_s, yb, yc_p, yc_s, gmix, wg, bg, wa, wb, wc, wo, gffn, wq, sk)


def _top16(x, iota, payloads=()):
    n = x.shape[0]
    vals, idxs = [], []
    picked = [[] for _ in payloads]
    for _ in range(PK_TOPK):
        m = jnp.max(x, axis=0, keepdims=True)
        idx = jnp.min(jnp.where(x == m, iota, float(n)), axis=0, keepdims=True)
        hit = iota == idx
        vals.append(m)
        idxs.append(idx)
        for lst, p in zip(picked, payloads):
            lst.append(jnp.sum(jnp.where(hit, p, 0.0), axis=0, keepdims=True))
        x = jnp.where(hit, -jnp.inf, x)
    return vals, idxs, picked


_KEY_LISTS = 4


def _top16_keys(x):
    n, tt = x.shape
    rows = n // _KEY_LISTS
    base = lax.broadcasted_iota(jnp.int32, (rows, tt), 0).astype(F32)
    val = [x[l * rows:(l + 1) * rows] for l in range(_KEY_LISTS)]
    idx = [base + float(l * rows) for l in range(_KEY_LISTS)]
    for a in [p for end in range(_KEY_LISTS - 1, 0, -1) for p in range(end)]:
        swap = val[a] < val[a + 1]
        val[a], val[a + 1] = jnp.where(swap, val[a + 1], val[a]), jnp.where(swap, val[a], val[a + 1])
        idx[a], idx[a + 1] = jnp.where(swap, idx[a + 1], idx[a]), jnp.where(swap, idx[a], idx[a + 1])
    vals, idxs = [], []
    for _ in range(PK_TOPK):
        m = jnp.max(val[0], axis=0, keepdims=True)
        sel = jnp.min(jnp.where(val[0] == m, idx[0], float(n)), axis=0, keepdims=True)
        hit = idx[0] == sel
        vals.append(m)
        idxs.append(sel)
        for l in range(_KEY_LISTS - 1):
            val[l] = jnp.where(hit, val[l + 1], val[l])
            idx[l] = jnp.where(hit, idx[l + 1], idx[l])
        val[-1] = jnp.where(hit, -jnp.inf, val[-1])
    return vals, idxs


_PAIRS = [(a, b) for a in range(PK_TOPK) for b in range(PK_TOPK) if (a + 1) * (b + 1) <= PK_TOPK]
_PAIR_ROWS = -(-len(_PAIRS) // SUBLANES) * SUBLANES


def _select_head(s0, s1):
    tt = s0.shape[1]
    iota_c = lax.broadcasted_iota(jnp.int32, (_PAIR_ROWS, tt), 0).astype(F32)
    pad = _PAIR_ROWS - len(_PAIRS)
    neg_row = jnp.full((1, tt), -jnp.inf, F32)
    zero_row = jnp.zeros((1, tt), F32)
    v0, i0 = _top16_keys(s0)
    v1, i1 = _top16_keys(s1)
    cand = jnp.concatenate([v0[a] + v1[b] for a, b in _PAIRS] + [neg_row] * pad, axis=0)
    cand_i = jnp.concatenate([i0[a] for a, _ in _PAIRS] + [zero_row] * pad, axis=0)
    cand_j = jnp.concatenate([i1[b] for _, b in _PAIRS] + [zero_row] * pad, axis=0)
    tv, _, (ti, tj) = _top16(cand, iota_c, (cand_i, cand_j))
    e = jnp.exp(jnp.concatenate(tv, axis=0) - tv[0])
    gates = e / jnp.sum(e, axis=0, keepdims=True)
    return jnp.concatenate(ti, axis=0), jnp.concatenate(tj, axis=0), gates


PEER_TILES = PEER_TB // TOPK_TT
PEER_UNITS_PER_STEP = PEER_TILES * PEER_HEADS // (2 * PEER_NC)
PEER_NSEL = PEER_HEADS * PK_TOPK
PEER_SLAB_W = PEER_SLAB_ROWS * N_KEYS


def _peer_kernel(hn_ref, h_ref, *refs):
    st_refs = refs[:PEER_UNITS_PER_STEP]
    uv_ref, o_ref, hs_ref, gm_ref, sel_ref, stage_ref = refs[PEER_UNITS_PER_STEP:]
    b = pl.program_id(0)
    s = pl.program_id(1)
    tb = hn_ref.shape[0]
    rpc = PEER_SLAB_ROWS
    sps = PEER_SLABS_PER_STEP
    slot_new = b % 2
    slot_cur = 1 - slot_new
    first = b == 0
    last = b == pl.num_programs(0) - 1
    middle = jnp.logical_not(jnp.logical_or(first, last))

    def select_step():
        rows = None
        for k, st_ref in enumerate(st_refs):
            unit = s * PEER_UNITS_PER_STEP + k
            s0, s1 = st_ref[0], st_ref[1]
            if rows is not None:
                anchor = 0.0 * rows[2][0:1, :]
                s0, s1 = s0 + anchor, s1 + anchor
            rows = _select_head(s0, s1)
            r0 = pl.multiple_of((unit % PEER_HEADS) * PK_TOPK, PK_TOPK)
            for x, val in enumerate(rows):
                stage_ref[(unit // PEER_HEADS) % 2, x, pl.ds(r0, PK_TOPK), :] = val

    def scores_pass():
        for k0 in range(0, sps, PEER_SLABS_PER_DOT):
            cols = slice(k0 * PEER_SLAB_W, (k0 + PEER_SLABS_PER_DOT) * PEER_SLAB_W)
            hc = _nt_dot(hn_ref[...], uv_ref[cols, :])
            for k in range(PEER_SLABS_PER_DOT):
                slab = hs_ref.at[s * sps + k0 + k]
                for ib in range(rpc):
                    col = (k * rpc + ib) * N_KEYS
                    slab[pl.ds(ib, tb, stride=rpc), :] = hc[:, col:col + N_KEYS]

    def combine_pass():
        c = s - PEER_NC
        for k0 in range(0, sps, PEER_SLABS_PER_DOT):
            rows = slice(k0 * PEER_SLAB_W, (k0 + PEER_SLABS_PER_DOT) * PEER_SLAB_W)
            wc = jnp.concatenate([hs_ref.at[c * sps + k0 + k][pl.ds(ib, tb, stride=rpc), :]
                                  for k in range(PEER_SLABS_PER_DOT) for ib in range(rpc)], axis=1)
            o_ref[...] += jnp.dot(wc.astype(BF16), uv_ref[rows, :], preferred_element_type=F32)

    @pl.when(first)
    def _():
        select_step()

    @pl.when(jnp.logical_and(s < PEER_NC, middle))
    def _():
        select_step()
        scores_pass()

    @pl.when(jnp.logical_and(s < PEER_NC, last))
    def _():
        scores_pass()

    @pl.when(jnp.logical_and(s == PEER_NC - 1, jnp.logical_not(first)))
    def _weights():
        sub = lax.broadcasted_iota(jnp.int32, (N_KEYS, N_KEYS), 0).astype(F32).astype(BF16)
        one = jnp.ones((N_KEYS, N_KEYS), BF16)
        zero = jnp.zeros((N_KEYS, N_KEYS), BF16)

        def gate_matrix(n):
            irow = sel_ref[slot_cur, 0, pl.ds(n, 1), :].astype(BF16)
            jrow = sel_ref[slot_cur, 1, pl.ds(n, 1), :].astype(BF16)
            g_hi, g_lo = _split_bf16(0.5 * sel_ref[slot_cur, 2, pl.ds(n, 1), :])
            at = jnp.where(sub == irow, one, zero)
            jhit = sub == jrow
            b_hi = jnp.where(jhit, jnp.broadcast_to(g_hi, jhit.shape), zero)
            b_lo = jnp.where(jhit, jnp.broadcast_to(g_lo, jhit.shape), zero)
            return _nt_dot(jnp.concatenate([at, at], axis=1),
                           jnp.concatenate([b_hi, b_lo], axis=1))

        unroll = PEER_TOKEN_UNROLL
        n_groups = tb // unroll

        def make_gates(group, slot):
            group = jnp.minimum(group, n_groups - 1)
            for t in range(unroll):
                gm_ref[slot, t * N_KEYS:(t + 1) * N_KEYS, :] = gate_matrix(group * unroll + t)

        def apply_gates(group, slot):
            base = pl.multiple_of(group * (unroll * rpc), unroll * rpc)
            for c in range(PEER_N_SLABS):
                hv = hs_ref[c, pl.ds(base, unroll * rpc), :]
                gm = jnp.concatenate([gm_ref[slot, t * N_KEYS + c * rpc:t * N_KEYS + (c + 1) * rpc, :]
                                      for t in range(unroll)], axis=0)
                hs_ref[c, pl.ds(base, unroll * rpc), :] = (hv * gm) * (1.0 + lax.erf(hv * (2.0 ** -0.5)))

        def body(m, carry):
            apply_gates(2 * m, 0)
            make_gates(2 * m + 1, 1)
            apply_gates(2 * m + 1, 1)
            make_gates(2 * m + 2, 0)
            return carry

        o_ref[...] = h_ref[...]
        make_gates(0, 0)
        lax.fori_loop(0, n_groups // 2, body, 0)

    @pl.when(jnp.logical_and(s >= PEER_NC, middle))
    def _():
        select_step()
        combine_pass()

    @pl.when(jnp.logical_and(s >= PEER_NC, last))
    def _():
        combine_pass()

    for k in range(PEER_UNITS_PER_STEP):
        unit = s * PEER_UNITS_PER_STEP + k

        @pl.when(jnp.logical_and(unit % PEER_HEADS == PEER_HEADS - 1, jnp.logical_not(last)))
        def _publish():
            tile = unit // PEER_HEADS
            t0 = pl.multiple_of(tile * TOPK_TT, TOPK_TT)
            for x in range(3):
                sel_ref[slot_new, x, pl.ds(t0, TOPK_TT), :] = stage_ref[tile % 2, x].T


def _cast_tables_kernel(u_ref, v_ref, o_ref):
    o_ref[0] = u_ref[...].astype(BF16)
    o_ref[1] = v_ref[...].astype(BF16)


def _cast_tables(u, v):
    depth, e, d = u.shape
    rc = PEER_EC // 2
    src = pl.BlockSpec((None, rc, d), lambda l, r: (l, r, 0))
    return pl.pallas_call(
        _cast_tables_kernel,
        grid=(depth, e // rc),
        in_specs=[src, src],
        out_specs=pl.BlockSpec((None, 2, rc, d), lambda l, r: (l, 0, r, 0)),
        out_shape=jax.ShapeDtypeStruct((depth, 2, e, d), BF16),
        compiler_params=_cparams(("arbitrary", "arbitrary")),
        name="cast_tables",
    )(u, v)


def _peer(hn, h, st, uv, layer):
    n = hn.shape[0]
    tb = PEER_TB
    nb = n // tb
    nc = PEER_NC
    row = lambda w: pl.BlockSpec((tb, w), lambda b, s: (jnp.maximum(b - 1, 0), 0))

    def st_spec(k):
        def index(b, s):
            unit = s * PEER_UNITS_PER_STEP + k
            return unit % PEER_HEADS, 0, jnp.minimum(b, nb - 1) * PEER_TILES + unit // PEER_HEADS
        return pl.BlockSpec((2, N_KEYS, TOPK_TT), index)

    return pl.pallas_call(
        _peer_kernel,
        grid=(nb + 1, 2 * nc),
        in_specs=[row(D_MODEL), row(D_MODEL)] + [st_spec(k) for k in range(PEER_UNITS_PER_STEP)] + [
                  pl.BlockSpec((None, None, PEER_EC, D_MODEL), lambda b, s: (layer, s // nc, s % nc, 0))],
        out_specs=row(D_MODEL),
        out_shape=jax.ShapeDtypeStruct((n, D_MODEL), F32),
        scratch_shapes=[pltpu.VMEM((PEER_N_SLABS, tb * PEER_SLAB_ROWS, N_KEYS), F32),
                        pltpu.VMEM((2, PEER_TOKEN_UNROLL * N_KEYS, N_KEYS), F32),
                        pltpu.VMEM((2, 3, tb, PEER_NSEL), F32),
                        pltpu.VMEM((2, 3, PEER_NSEL, TOPK_TT), F32)],
        compiler_params=pltpu.CompilerParams(dimension_semantics=("arbitrary", "arbitrary"),
                                             vmem_limit_bytes=PEER_VMEM_LIMIT),
        name="peer_experts",
    )(hn, h, *([st] * PEER_UNITS_PER_STEP), uv)


def _bias_kernel(g_ref, full_ref, band_ref):
    _, nq, nk = full_ref.shape
    w = g_ref.shape[1]
    qc = lax.broadcasted_iota(jnp.int32, (nq, nk), 0) // CHUNK
    kc = lax.broadcasted_iota(jnp.int32, (nq, nk), 1) // CHUNK
    band = jnp.abs(2 * (kc - qc) - BAND // CHUNK) <= BAND // CHUNK
    for h in range(ATT_HEADS):
        line = jnp.broadcast_to(g_ref[h:h + 1, :], (nq, w))
        t = pltpu.roll(line, w - nq + 1, 1, stride=1, stride_axis=0)[:, :nk]
        full_ref[h] = t
        band_ref[h] = jnp.where(band, t, NEG_BIG)


def _rel_bias(table, nq, nk):
    w = BIAS_LINE_W
    assert nq + nk - 1 <= w
    dist = nk - 1 - jnp.arange(w)
    line = table[:, jnp.clip(dist, -REL_CLIP, REL_CLIP) + REL_CLIP].astype(F32)
    shape = jax.ShapeDtypeStruct((ATT_HEADS, nq, nk), F32)
    return pl.pallas_call(
        _bias_kernel,
        out_shape=[shape, shape],
        compiler_params=pltpu.CompilerParams(vmem_limit_bytes=VMEM_LIMIT),
        name="rel_bias",
    )(line)


def _block_diag(blocks):
    g, r, c = blocks.shape
    eye = jnp.eye(g, dtype=blocks.dtype)
    return (eye[:, None, :, None] * blocks[:, :, None, :]).reshape(g * r, g * c)


def _sgu_mats(w_s, b_s, length):
    g = w_s.shape[0]
    reps = SGU_CHUNK // length
    tri = jnp.tril(jnp.ones((length, length), bool))
    ws = jnp.where(tri[None], w_s[:, :length, :length], 0.0)
    eye = jnp.eye(reps, dtype=ws.dtype)
    mats = (eye[None, :, None, :, None] * ws[:, None, :, None, :]).reshape(g, SGU_CHUNK, SGU_CHUNK)
    bias = jnp.tile(b_s[:, :length], (1, reps))
    bias = jnp.repeat(bias.T, SGU_GW, axis=1)
    return mats, bias


def kernel(x_prompt, x_sample, state_pool, cache_k, cache_v, g_mix, w_in, pool_w, pool_scale, sgu_norm, sgu_w, sgu_b, q_norm, k_norm, rel_table, w_gate, b_gate, w_br_a, w_br_b, w_br_c, w_out, g_ffn, peer_wq, peer_subkeys, peer_u, peer_v):
    depth = g_mix.shape[0]
    bsz, seq, d = x_prompt.shape
    dbsz, dseq, _ = x_sample.shape
    n_p, n_s = bsz * seq, dbsz * dseq
    n_cache = cache_k.shape[2]
    assert seq % ATT_QB == 0 and n_p % ROW_TILE == 0 and n_s % ROW_TILE == 0
    assert dseq <= SGU_CHUNK and SGU_CHUNK % dseq == 0 and dseq >= POOL_STATE + 1 and dseq % SUBLANES == 0
    assert n_cache == 2 * ATT_QB and dseq <= ATT_QB
    n_tot = -(-(n_p + n_s) // PEER_TB) * PEER_TB
    n_prompt_tiles = n_p // ROW_TILE
    xa, xb, xb_tile0 = x_prompt.reshape(n_p, d), x_sample.reshape(n_s, d), 0

    hsum = _block_diag(jnp.full((ATT_HEADS, HEAD_DIM, HEAD_DIM), 1.0 / HEAD_DIM, F32)).astype(BF16)
    zero_prefix = jnp.zeros((bsz, HALO, POOL_W), F32)
    uv_tables = _cast_tables(peer_u, peer_v)
    keep = min(BAND, seq)
    outs = {name: [] for name in ("pool_p", "pool_s", "k_p", "v_p", "k_s", "v_s", "sgu_s")}

    for l in range(depth):
        row = lambda a: a[l].reshape(1, -1)
        m_p, b_p = _sgu_mats(sgu_w[l], sgu_b[l], SGU_CHUNK)
        m_s, b_s = _sgu_mats(sgu_w[l], sgu_b[l], dseq)
        ms = jnp.stack([m_p, m_s]).astype(BF16)
        bs = jnp.stack([b_p, b_s])
        a, yb, vn, q, k, v = _in_proj(
            xa, xb, xb_tile0, n_tot, row(g_mix), w_in[l].astype(BF16), row(sgu_norm),
            jnp.tile(q_norm[l], ATT_HEADS).reshape(1, -1), jnp.tile(k_norm[l], ATT_HEADS).reshape(1, -1),
            hsum, ms, bs, n_prompt_tiles)

        pw = _block_diag(pool_w[l]).astype(BF16)
        sample_prefix = jnp.pad(state_pool[l], ((0, 0), (HALO - POOL_STATE, 0), (0, 0)))
        ya_p = _pool(a, zero_prefix, pw, row(pool_scale), 0, bsz, seq, min(POOL_TILE, seq))
        ya_s = _pool(a, sample_prefix, pw, row(pool_scale), n_p, dbsz, dseq, dseq)

        bias_full, bias_band = _rel_bias(rel_table[l], ATT_QB, 3 * ATT_QB)
        yc_p = _attn_prompt(q, k, v, bias_band, bsz, seq)
        yc_s = _attn_sample(q, k, v, cache_k[l].reshape(dbsz, n_cache, ATT_W),
                            cache_v[l].reshape(dbsz, n_cache, ATT_W),
                            bias_full[:, :dseq, :n_cache + dseq], n_p, dbsz, dseq)

        h, hn, st = _merge(
            xa, xb, xb_tile0, n_tot, n_prompt_tiles, ya_p, ya_s, yb, yc_p, yc_s,
            row(g_mix), w_gate[l].astype(BF16), row(b_gate), w_br_a[l].astype(BF16),
            w_br_b[l].astype(BF16), w_br_c[l].astype(BF16), w_out[l].astype(BF16), row(g_ffn),
            peer_wq[l].astype(BF16), peer_subkeys[l].astype(BF16))
        x = _peer(hn, h, st, uv_tables, l)
        xa, xb, xb_tile0 = x, x, n_prompt_tiles

        def tail(arr, count):
            return jnp.stack([arr[(b + 1) * seq - count:(b + 1) * seq] for b in range(bsz)])

        sample = lambda arr: arr[n_p:n_p + n_s]
        outs["pool_p"].append(tail(a, POOL_STATE))
        outs["pool_s"].append(sample(a).reshape(dbsz, dseq, POOL_W)[:, dseq - POOL_STATE:])
        outs["k_p"].append(tail(k, keep).reshape(bsz, keep, ATT_HEADS, HEAD_DIM))
        outs["v_p"].append(tail(v, keep).reshape(bsz, keep, ATT_HEADS, HEAD_DIM))
        outs["k_s"].append(sample(k).reshape(dbsz, dseq, ATT_HEADS, HEAD_DIM))
        outs["v_s"].append(sample(v).reshape(dbsz, dseq, ATT_HEADS, HEAD_DIM))
        outs["sgu_s"].append(sample(vn).reshape(dbsz, dseq, SGU_W))

    st = lambda name: jnp.stack(outs[name])
    return (x[:n_p].reshape(bsz, seq, d), x[n_p:n_p + n_s].reshape(dbsz, dseq, d), st("pool_p"), st("pool_s"),
            st("k_p"), st("v_p"), st("k_s"), st("v_s"), st("sgu_s"))
```
